```python
import jax, jax.numpy as jnp
from jax import lax
import numpy as np

D_MODEL = 1024
BATCH = 4
SEQ = 8192
DEPTH = 2

N_HEADS = 16
N_KV_HEADS = 4
HEAD_DIM = 64
GROUP = N_HEADS // N_KV_HEADS
Q_BLOCK = 128
ROPE_THETA = 10000.0
ROPE_HALF = HEAD_DIM // 2
GRID_W = 64
D_RNN = 1024
N_RNN_BLOCKS = 16
RNN_BLOCK = D_RNN // N_RNN_BLOCKS
CONV_W = 4
CONV_PAD = (2, 1)
LRU_C = 8.0
D_FF = 2816
FFN_RESID = 0.5
N_BRANCH = 2
EPS = 1e-6
Q_COLS = N_HEADS * HEAD_DIM
KV_COLS = N_KV_HEADS * HEAD_DIM
IN_COLS = Q_COLS + 2 * KV_COLS + 2 * D_RNN + N_BRANCH * D_MODEL
SPLITS = (Q_COLS, Q_COLS + KV_COLS, Q_COLS + 2 * KV_COLS,
          Q_COLS + 2 * KV_COLS + D_RNN, Q_COLS + 2 * KV_COLS + 2 * D_RNN)

kernel_name = "griffin_gated_gqa_rglru_macaron_encoder"


def rmsnorm(x, g):
    xf = x.astype(jnp.float32)
    y = xf * lax.rsqrt(jnp.mean(xf * xf, axis=-1, keepdims=True) + EPS)
    return (y * g.astype(jnp.float32)).astype(x.dtype)


def swiglu(x, w1, w2):
    gu = x @ w1
    g, u = jnp.split(gu, 2, axis=-1)
    return (jax.nn.silu(g) * u) @ w2


def rope_tables(seq):
    rows = seq // GRID_W
    row = jnp.repeat(jnp.arange(rows, dtype=jnp.int32), GRID_W).astype(jnp.float32)
    col = jnp.tile(jnp.arange(GRID_W, dtype=jnp.int32), rows).astype(jnp.float32)
    inv = ROPE_THETA ** (-jnp.arange(0, ROPE_HALF, 2, dtype=jnp.float32) / ROPE_HALF)
    ang_r = row[:, None] * inv[None, :]
    ang_c = col[:, None] * inv[None, :]
    return (jnp.cos(ang_r)[:, None], jnp.sin(ang_r)[:, None],
            jnp.cos(ang_c)[:, None], jnp.sin(ang_c)[:, None])


def apply_rope(x, tabs):
    cr, sr, cc, sc = tabs
    xf = x.astype(jnp.float32)
    r1, r2, c1, c2 = jnp.split(xf, 4, axis=-1)
    out = jnp.concatenate([r1 * cr - r2 * sr, r2 * cr + r1 * sr,
                           c1 * cc - c2 * sc, c2 * cc + c1 * sc], axis=-1)
    return out.astype(x.dtype)


def block_attention(q, k, v):
    b, s = q.shape[0], q.shape[1]
    nb = s // Q_BLOCK
    scale = HEAD_DIM ** -0.5
    qb = (q * scale).reshape(b, nb, Q_BLOCK, N_KV_HEADS, GROUP, HEAD_DIM).transpose(1, 0, 2, 3, 4, 5)

    def one_block(qi):
        sc = jnp.einsum('bqhgd,bkhd->bhgqk', qi, k).astype(jnp.float32)
        p = jax.nn.softmax(sc, axis=-1)
        return jnp.einsum('bhgqk,bkhd->bqhgd', p.astype(v.dtype), v)

    o = lax.map(one_block, qb)
    return o.transpose(1, 0, 2, 3, 4, 5).reshape(b, s, Q_COLS)


def depthwise_conv(x, w, bias):
    y = lax.conv_general_dilated(x, w[:, None, :], window_strides=(1,), padding=[CONV_PAD],
                                 dimension_numbers=('NWC', 'WIO', 'NWC'),
                                 feature_group_count=D_RNN)
    return y + bias


def _lin_combine(c1, c2):
    a1, b1 = c1
    a2, b2 = c2
    return a1 * a2, a2 * b1 + b2


def rg_lru(x, wa, ba, wx, bx, lam, reverse):
    b, s, _ = x.shape
    xb = x.reshape(b, s, N_RNN_BLOCKS, RNN_BLOCK)
    r = jax.nn.sigmoid((jnp.einsum('bsnc,ncd->bsnd', xb, wa).reshape(b, s, D_RNN) + ba).astype(jnp.float32))
    i = jax.nn.sigmoid((jnp.einsum('bsnc,ncd->bsnd', xb, wx).reshape(b, s, D_RNN) + bx).astype(jnp.float32))
    log_a = -LRU_C * r * jax.nn.softplus(-lam.astype(jnp.float32))
    a = jnp.exp(log_a)
    u = jnp.sqrt(-jnp.expm1(2.0 * log_a)) * (i * x.astype(jnp.float32))
    _, h = lax.associative_scan(_lin_combine, (a, u), axis=1, reverse=reverse)
    return h


def setup_inputs(seed: int = 0) -> dict:
    key = jax.random.key(seed)
    ks = jax.random.split(key, 24)
    f32 = jnp.float32

    def nrm(k, shape, scale):
        return jax.random.normal(k, shape, f32) * scale

    def gain(k, shape):
        return 1.0 + 0.02 * jax.random.normal(k, shape, f32)

    L = DEPTH
    u = jax.random.uniform(ks[16], (L, 2, D_RNN), f32, 0.9, 0.999)
    p = u ** (1.0 / LRU_C)
    lam = jnp.log(p) - jnp.log1p(-p)
    return {
        "x": jax.random.normal(ks[0], (BATCH, SEQ, D_MODEL), f32),
        "ffn1_norm": gain(ks[1], (L, D_MODEL)),
        "ffn1_w1": nrm(ks[2], (L, D_MODEL, 2 * D_FF), D_MODEL ** -0.5),
        "ffn1_w2": nrm(ks[3], (L, D_FF, D_MODEL), D_FF ** -0.5),
        "mix_norm": gain(ks[4], (L, D_MODEL)),
        "w_in": nrm(ks[5], (L, D_MODEL, IN_COLS), D_MODEL ** -0.5),
        "b_gate": nrm(ks[6], (L, N_BRANCH, D_MODEL), 0.01),
        "q_norm": gain(ks[7], (L, HEAD_DIM)),
        "k_norm": gain(ks[8], (L, HEAD_DIM)),
        "w_attn_o": nrm(ks[9], (L, Q_COLS, D_MODEL), Q_COLS ** -0.5),
        "conv_w": nrm(ks[10], (L, CONV_W, D_RNN), CONV_W ** -0.5),
        "conv_b": nrm(ks[11], (L, D_RNN), 0.01),
        "lru_wa": nrm(ks[12], (L, 2, N_RNN_BLOCKS, RNN_BLOCK, RNN_BLOCK), RNN_BLOCK ** -0.5),
        "lru_ba": nrm(ks[13], (L, 2, D_RNN), 0.01),
        "lru_wx": nrm(ks[14], (L, 2, N_RNN_BLOCKS, RNN_BLOCK, RNN_BLOCK), RNN_BLOCK ** -0.5),
        "lru_bx": nrm(ks[15], (L, 2, D_RNN), 0.01),
        "lru_lambda": lam,
        "w_rnn_o": nrm(ks[17], (L, D_RNN, D_MODEL), D_RNN ** -0.5),
        "w_out": nrm(ks[18], (L, D_MODEL, D_MODEL), D_MODEL ** -0.5),
        "ffn2_norm": gain(ks[19], (L, D_MODEL)),
        "ffn2_w1": nrm(ks[20], (L, D_MODEL, 2 * D_FF), D_MODEL ** -0.5),
        "ffn2_w2": nrm(ks[21], (L, D_FF, D_MODEL), D_FF ** -0.5),
    }


def reference(x, ffn1_norm, ffn1_w1, ffn1_w2, mix_norm, w_in, b_gate, q_norm, k_norm,
              w_attn_o, conv_w, conv_b, lru_wa, lru_ba, lru_wx, lru_bx, lru_lambda,
              w_rnn_o, w_out, ffn2_norm, ffn2_w1, ffn2_w2):
    b, s, _ = x.shape
    tabs = rope_tables(s)
    for l in range(DEPTH):
        x = x + FFN_RESID * swiglu(rmsnorm(x, ffn1_norm[l]), ffn1_w1[l], ffn1_w2[l])

        h = rmsnorm(x, mix_norm[l])
        proj = h @ w_in[l]
        q, k, v, xr, yr, gl = jnp.split(proj, SPLITS, axis=-1)

        q = apply_rope(rmsnorm(q.reshape(b, s, N_HEADS, HEAD_DIM), q_norm[l]), tabs)
        k = apply_rope(rmsnorm(k.reshape(b, s, N_KV_HEADS, HEAD_DIM), k_norm[l]), tabs)
        v = v.reshape(b, s, N_KV_HEADS, HEAD_DIM)
        attn = block_attention(q, k, v) @ w_attn_o[l]

        xc = depthwise_conv(xr, conv_w[l], conv_b[l])
        hr = (rg_lru(xc, lru_wa[l, 0], lru_ba[l, 0], lru_wx[l, 0], lru_bx[l, 0], lru_lambda[l, 0], False)
              + rg_lru(xc, lru_wa[l, 1], lru_ba[l, 1], lru_wx[l, 1], lru_bx[l, 1], lru_lambda[l, 1], True))
        rnn = (hr.astype(x.dtype) * jax.nn.gelu(yr)) @ w_rnn_o[l]

        gates = jax.nn.sigmoid(gl.reshape(b, s, N_BRANCH, D_MODEL) + b_gate[l])
        merged = gates[:, :, 0] * attn + gates[:, :, 1] * rnn
        x = x + merged @ w_out[l]

        x = x + FFN_RESID * swiglu(rmsnorm(x, ffn2_norm[l]), ffn2_w1[l], ffn2_w2[l])
    return x
```

```python
import functools
import math

import jax
import jax.numpy as jnp
from jax import lax
from jax.experimental import pallas as pl
from jax.experimental.pallas import tpu as pltpu

F32 = jnp.float32
BF16 = jnp.bfloat16

N_HEADS = 16
N_KV_HEADS = 4
HEAD_DIM = 64
GROUP = N_HEADS // N_KV_HEADS
ROPE_THETA = 10000.0
ROPE_QUARTER = HEAD_DIM // 4
GRID_W = 64
N_RNN_BLOCKS = 16
CONV_W = 4
CONV_LEFT = 2
LRU_C = 8.0
FFN_RESID = 0.5
EPS = 1e-6
LOG2E = math.log2(math.e)

LANES = 128
SUBLANES = 8
MXU_DIM = 256
VMEM_LIMIT_BYTES = 56 * 1024 * 1024

FFN_TM = 1024
FFN_TF = 256
PROJ_TM = 256
ATT_TQ = 256
ATT_TK = 512
RNN_TM = 512
OUT_TM = 512


def _params(*sem):
    return pltpu.CompilerParams(dimension_semantics=sem, vmem_limit_bytes=VMEM_LIMIT_BYTES)


def _rms_scale(x):
    return lax.rsqrt(jnp.mean(x * x, axis=-1, keepdims=True) + EPS)


def _sigmoid(x):
    return 1.0 / (1.0 + jnp.exp(-x))


def _ffn_kernel(x_ref, g_ref, w1g_ref, w1u_ref, w2_ref, o_ref, h_ref, acc_ref):
    f = pl.program_id(1)

    @pl.when(f == 0)
    def _():
        x = x_ref[...]
        h_ref[...] = (x * _rms_scale(x) * g_ref[...]).astype(BF16)
        acc_ref[...] = jnp.zeros_like(acc_ref)

    h = h_ref[...]
    g = jnp.dot(h, w1g_ref[...], preferred_element_type=F32)
    u = jnp.dot(h, w1u_ref[...], preferred_element_type=F32)
    act = (g * _sigmoid(g) * u).astype(BF16)
    acc_ref[...] += jnp.dot(act, w2_ref[...], preferred_element_type=F32)

    @pl.when(f == pl.num_programs(1) - 1)
    def _():
        o_ref[...] = x_ref[...] + FFN_RESID * acc_ref[...]


def _ffn(x2, gain, w1, w2):
    t, d = x2.shape
    d_ff = w2.shape[0]
    tm = min(FFN_TM, t)
    tf = FFN_TF
    nf = d_ff // tf
    return pl.pallas_call(
        _ffn_kernel,
        grid=(t // tm, nf),
        in_specs=[
            pl.BlockSpec((tm, d), lambda i, f: (i, 0)),
            pl.BlockSpec((1, d), lambda i, f: (0, 0)),
            pl.BlockSpec((d, tf), lambda i, f: (0, f)),
            pl.BlockSpec((d, tf), lambda i, f: (0, nf + f)),
            pl.BlockSpec((tf, d), lambda i, f: (f, 0)),
        ],
        out_specs=pl.BlockSpec((tm, d), lambda i, f: (i, 0)),
        out_shape=jax.ShapeDtypeStruct((t, d), F32),
        scratch_shapes=[pltpu.VMEM((tm, d), BF16), pltpu.VMEM((tm, d), F32)],
        compiler_params=_params("parallel", "arbitrary"),
        name="ffn",
    )(x2, gain.reshape(1, d), w1, w1, w2)


def _rope_tables(seq):
    pos = jnp.arange(seq, dtype=jnp.int32)
    row = (pos // GRID_W).astype(F32)
    col = (pos % GRID_W).astype(F32)
    half = HEAD_DIM // 2
    inv = ROPE_THETA ** (-jnp.arange(0, half, 2, dtype=F32) / half)
    ang_r = row[:, None] * inv[None, :]
    ang_c = col[:, None] * inv[None, :]
    zero = jnp.zeros_like(ang_r)
    cos = jnp.concatenate([jnp.cos(ang_r), jnp.cos(ang_r), jnp.cos(ang_c), jnp.cos(ang_c)], axis=1)
    s_up = jnp.concatenate([-jnp.sin(ang_r), zero, -jnp.sin(ang_c), zero], axis=1)
    s_dn = jnp.concatenate([zero, jnp.sin(ang_r), zero, jnp.sin(ang_c)], axis=1)
    rep = LANES // HEAD_DIM
    return tuple(jnp.tile(a, (1, rep)) for a in (cos, s_up, s_dn))


def _head_norm_rope(z, seg, gain, cos, s_up, s_dn):
    width = z.shape[1]
    sq = (z * z).astype(BF16)
    ms = jnp.concatenate(
        [jnp.dot(sq[:, c:c + MXU_DIM], seg, preferred_element_type=F32) for c in range(0, width, MXU_DIM)],
        axis=1)
    zn = z * lax.rsqrt(ms + EPS) * gain
    rep = width // LANES
    widen = lambda a: jnp.concatenate([a] * rep, axis=1)
    up = pltpu.roll(zn, width - ROPE_QUARTER, axis=1)
    dn = pltpu.roll(zn, ROPE_QUARTER, axis=1)
    return zn * widen(cos) + up * widen(s_up) + dn * widen(s_dn)


def _proj_kernel(x_ref, gn_ref, wq_ref, wk_ref, wv_ref, wx_ref, wy_ref, wg_ref, bg_ref,
                 gq_ref, gk_ref, seg_ref, cos_ref, sup_ref, sdn_ref,
                 q_ref, k_ref, v_ref, xr_ref, gy_ref, gate_ref):
    x = x_ref[0]
    h = (x * _rms_scale(x) * gn_ref[...]).astype(BF16)
    seg = seg_ref[...]
    cos, s_up, s_dn = cos_ref[...], sup_ref[...], sdn_ref[...]

    q = jnp.dot(h, wq_ref[...], preferred_element_type=F32)
    qscale = HEAD_DIM ** -0.5 * LOG2E
    q_ref[0] = (_head_norm_rope(q, seg, gq_ref[...], cos, s_up, s_dn) * qscale).astype(BF16)

    k = jnp.dot(h, wk_ref[...], preferred_element_type=F32)
    k = _head_norm_rope(k, seg, gk_ref[...], cos, s_up, s_dn).astype(BF16)
    v = jnp.dot(h, wv_ref[...], preferred_element_type=F32).astype(BF16)
    for j in range(N_KV_HEADS):
        k_ref[0, j] = k[:, j * HEAD_DIM:(j + 1) * HEAD_DIM]
        v_ref[0, j] = v[:, j * HEAD_DIM:(j + 1) * HEAD_DIM]

    xr_ref[0] = jnp.dot(h, wx_ref[...], preferred_element_type=F32)

    y = jnp.dot(h, wy_ref[...], preferred_element_type=F32)
    gelu = 0.5 * y * (1.0 + jnp.tanh(math.sqrt(2.0 / math.pi) * (y + 0.044715 * (y * y * y))))
    gy_ref[0] = gelu.astype(BF16)

    gl = jnp.dot(h, wg_ref[...], preferred_element_type=F32) + bg_ref[...]
    gate_ref[0] = _sigmoid(gl).astype(BF16)


def _in_proj(x, gain, w_in, b_gate, q_gain, k_gain):
    b, s, d = x.shape
    q_cols = N_HEADS * HEAD_DIM
    kv_cols = N_KV_HEADS * HEAD_DIM
    d_rnn = (w_in.shape[1] - q_cols - 2 * kv_cols - 2 * d) // 2
    o = [0, q_cols, q_cols + kv_cols, q_cols + 2 * kv_cols, q_cols + 2 * kv_cols + d_rnn,
         q_cols + 2 * kv_cols + 2 * d_rnn, w_in.shape[1]]
    ws = [w_in[:, o[i]:o[i + 1]] for i in range(6)]
    tm = min(PROJ_TM, s)
    nt = s // tm
    cos, s_up, s_dn = _rope_tables(s)
    lane = jnp.arange(MXU_DIM)
    seg = jnp.where(lane[:, None] // HEAD_DIM == lane[None, :] // HEAD_DIM, 1.0 / HEAD_DIM, 0.0).astype(BF16)
    const = lambda shape: pl.BlockSpec(shape, lambda bi, ti: (0,) * len(shape))
    tab = pl.BlockSpec((tm, LANES), lambda bi, ti: (ti, 0))
    row = lambda width: pl.BlockSpec((1, tm, width), lambda bi, ti: (bi, ti, 0))
    head = pl.BlockSpec((1, N_KV_HEADS, tm, HEAD_DIM), lambda bi, ti: (bi, 0, ti, 0))
    return pl.pallas_call(
        _proj_kernel,
        grid=(b, nt),
        in_specs=[row(d), const((1, d))] + [const(w.shape) for w in ws]
        + [const((1, 2 * d)), const((1, q_cols)), const((1, kv_cols)), const((MXU_DIM, MXU_DIM)), tab, tab, tab],
        out_specs=[row(q_cols), head, head, row(d_rnn), row(d_rnn), row(2 * d)],
        out_shape=[
            jax.ShapeDtypeStruct((b, s, q_cols), BF16),
            jax.ShapeDtypeStruct((b, N_KV_HEADS, s, HEAD_DIM), BF16),
            jax.ShapeDtypeStruct((b, N_KV_HEADS, s, HEAD_DIM), BF16),
            jax.ShapeDtypeStruct((b, s, d_rnn), F32),
            jax.ShapeDtypeStruct((b, s, d_rnn), BF16),
            jax.ShapeDtypeStruct((b, s, 2 * d), BF16),
        ],
        compiler_params=_params("parallel", "parallel"),
        name="in_proj",
    )(x, gain.reshape(1, d), *ws, b_gate.reshape(1, 2 * d),
      jnp.tile(q_gain, N_HEADS).reshape(1, q_cols), jnp.tile(k_gain, N_KV_HEADS).reshape(1, kv_cols),
      seg, cos, s_up, s_dn)


def _attn_kernel(q_ref, k_ref, v_ref, o_ref, qs_ref, m_ref, l_ref, acc_ref, *, tk):
    tq = q_ref.shape[1]
    seq = k_ref.shape[2]
    q = q_ref[0]
    for g in range(GROUP):
        qs_ref[g * tq:(g + 1) * tq, :] = q[:, g * HEAD_DIM:(g + 1) * HEAD_DIM]
    m_ref[...] = jnp.full_like(m_ref, -jnp.inf)
    l_ref[...] = jnp.zeros_like(l_ref)
    acc_ref[...] = jnp.zeros_like(acc_ref)
    rep = tk // LANES

    def body(c, carry):
        start = pl.multiple_of(c * tk, tk)
        k = k_ref[0, 0, pl.ds(start, tk), :]
        v = v_ref[0, 0, pl.ds(start, tk), :]
        s = lax.dot_general(qs_ref[...], k, (((1,), (1,)), ((), ())), preferred_element_type=F32)
        m_prev = m_ref[...]
        m_new = jnp.maximum(m_prev, jnp.max(s, axis=1, keepdims=True))
        alpha = jnp.exp2(m_prev - m_new)
        p = jnp.exp2(s - jnp.concatenate([m_new] * rep, axis=1))
        l_ref[...] = alpha * l_ref[...] + jnp.sum(p, axis=1, keepdims=True)
        acc_ref[...] = alpha[:, :HEAD_DIM] * acc_ref[...] + jnp.dot(p.astype(BF16), v, preferred_element_type=F32)
        m_ref[...] = m_new
        return carry

    lax.fori_loop(0, seq // tk, body, 0)
    out = acc_ref[...] / l_ref[:, :HEAD_DIM]
    for g in range(GROUP):
        o_ref[0, :, g * HEAD_DIM:(g + 1) * HEAD_DIM] = out[g * tq:(g + 1) * tq, :].astype(o_ref.dtype)


def _attention(q, k, v):
    b, s, q_cols = q.shape
    tq = min(ATT_TQ, s)
    tk = min(ATT_TK, s)
    gw = GROUP * HEAD_DIM
    kv_spec = pl.BlockSpec((1, 1, s, HEAD_DIM), lambda bi, j, i: (bi, j, 0, 0))
    return pl.pallas_call(
        functools.partial(_attn_kernel, tk=tk),
        grid=(b, N_KV_HEADS, s // tq),
        in_specs=[pl.BlockSpec((1, tq, gw), lambda bi, j, i: (bi, i, j)), kv_spec, kv_spec],
        out_specs=pl.BlockSpec((1, tq, gw), lambda bi, j, i: (bi, i, j)),
        out_shape=jax.ShapeDtypeStruct((b, s, q_cols), BF16),
        scratch_shapes=[
            pltpu.VMEM((GROUP * tq, HEAD_DIM), BF16),
            pltpu.VMEM((GROUP * tq, LANES), F32),
            pltpu.VMEM((GROUP * tq, LANES), F32),
            pltpu.VMEM((GROUP * tq, HEAD_DIM), F32),
        ],
        compiler_params=_params("parallel", "parallel", "arbitrary"),
        name="attention",
    )(q, k, v)


def _rnn_kernel(*refs, reverse):
    if reverse:
        (xr_ref, prev_ref, next_ref, cw_ref, cb_ref, wcat_ref, ba_ref, bx_ref, lam_ref, hf_ref, gy_ref,
         o_ref, xpad_ref, a_ref, u_ref, hs_ref, hc_ref) = refs
    else:
        (xr_ref, prev_ref, next_ref, cw_ref, cb_ref, wcat_ref, ba_ref, bx_ref, lam_ref,
         o_ref, xpad_ref, a_ref, u_ref, hs_ref, hc_ref) = refs
    t = pl.program_id(1)
    nt = pl.num_programs(1)
    ti = nt - 1 - t if reverse else t
    tm = xr_ref.shape[1]
    d = xr_ref.shape[2]
    halo = SUBLANES

    @pl.when(t == 0)
    def _():
        hc_ref[...] = jnp.zeros_like(hc_ref)

    xpad_ref[halo:halo + tm, :] = xr_ref[0]
    xpad_ref[0:halo, :] = jnp.where(ti > 0, prev_ref[0], 0.0)
    xpad_ref[halo + tm:halo + tm + halo, :] = jnp.where(ti < nt - 1, next_ref[0], 0.0)
    xc = cb_ref[...] + sum(
        cw_ref[kk:kk + 1, :] * xpad_ref[halo - CONV_LEFT + kk:halo - CONV_LEFT + kk + tm, :]
        for kk in range(CONV_W))

    xcb = xc.astype(BF16)
    lam = lam_ref[...]
    softplus_neg_lam = jnp.maximum(-lam, 0.0) + jnp.log1p(jnp.exp(-jnp.abs(lam)))
    for c in range(d // MXU_DIM):
        sl = slice(c * MXU_DIM, (c + 1) * MXU_DIM)
        g = jnp.dot(xcb[:, sl], wcat_ref[c], preferred_element_type=F32)
        r = _sigmoid(g[:, :MXU_DIM] + ba_ref[:, sl])
        i = _sigmoid(g[:, MXU_DIM:] + bx_ref[:, sl])
        log_a = -LRU_C * r * softplus_neg_lam[:, sl]
        a = jnp.exp(log_a)
        a_ref[:, sl] = a
        u_ref[:, sl] = jnp.sqrt(1.0 - a * a) * (i * xc[:, sl])

    def group(gi, h):
        base = pl.multiple_of((tm // SUBLANES - 1 - gi if reverse else gi) * SUBLANES, SUBLANES)
        for j in range(SUBLANES):
            row = base + (SUBLANES - 1 - j if reverse else j)
            h = a_ref[pl.ds(row, 1), :] * h + u_ref[pl.ds(row, 1), :]
            hs_ref[pl.ds(row, 1), :] = h
        return h

    hc_ref[...] = lax.fori_loop(0, tm // SUBLANES, group, hc_ref[...])

    if reverse:
        o_ref[0] = ((hf_ref[0] + hs_ref[...]) * gy_ref[0].astype(F32)).astype(o_ref.dtype)
    else:
        o_ref[0] = hs_ref[...]


def _rnn_pass(xr, conv_w, conv_b, wa, ba, wx, bx, lam, reverse, hf=None, gy=None):
    b, s, d = xr.shape
    tm = min(RNN_TM, s)
    nt = s // tm
    nb, blk, _ = wa.shape
    per = MXU_DIM // blk
    nslab = d // MXU_DIM

    def slab_diag(w):
        w = w.reshape(nslab, per, blk, blk)
        eye = jnp.eye(per, dtype=w.dtype)
        return jnp.einsum('spij,pq->spiqj', w, eye).reshape(nslab, MXU_DIM, MXU_DIM)

    wcat = jnp.concatenate([slab_diag(wa), slab_diag(wx)], axis=2).astype(BF16)
    tile_of = (lambda t: nt - 1 - t) if reverse else (lambda t: t)
    hb = tm // SUBLANES
    main = lambda width: pl.BlockSpec((1, tm, width), lambda bi, t: (bi, tile_of(t), 0))
    const = lambda shape: pl.BlockSpec(shape, lambda bi, t: (0,) * len(shape))
    in_specs = [
        main(d),
        pl.BlockSpec((1, SUBLANES, d), lambda bi, t: (bi, jnp.maximum(tile_of(t) * hb - 1, 0), 0)),
        pl.BlockSpec((1, SUBLANES, d), lambda bi, t: (bi, jnp.minimum((tile_of(t) + 1) * hb, s // SUBLANES - 1), 0)),
        const((CONV_W, d)), const((1, d)), const(wcat.shape), const((1, d)), const((1, d)), const((1, d)),
    ]
    args = [xr, xr, xr, conv_w, conv_b.reshape(1, d), wcat, ba.reshape(1, d), bx.reshape(1, d), lam.reshape(1, d)]
    if reverse:
        in_specs += [main(d), main(d)]
        args += [hf, gy]
    return pl.pallas_call(
        functools.partial(_rnn_kernel, reverse=reverse),
        grid=(b, nt),
        in_specs=in_specs,
        out_specs=main(d),
        out_shape=jax.ShapeDtypeStruct((b, s, d), BF16 if reverse else F32),
        scratch_shapes=[
            pltpu.VMEM((tm + 2 * SUBLANES, d), F32),
            pltpu.VMEM((tm, d), F32),
            pltpu.VMEM((tm, d), F32),
            pltpu.VMEM((tm, d), F32),
            pltpu.VMEM((1, d), F32),
        ],
        compiler_params=_params("parallel", "arbitrary"),
        name="rnn_bwd" if reverse else "rnn_fwd",
    )(*args)


def _merge_kernel(x_ref, attn_ref, rg_ref, gate_ref, wa_ref, wr_ref, wo_ref, o_ref):
    d = x_ref.shape[1]
    a = jnp.dot(attn_ref[...], wa_ref[...], preferred_element_type=F32)
    r = jnp.dot(rg_ref[...], wr_ref[...], preferred_element_type=F32)
    merged = gate_ref[:, :d].astype(F32) * a + gate_ref[:, d:].astype(F32) * r
    o_ref[...] = x_ref[...] + jnp.dot(merged.astype(BF16), wo_ref[...], preferred_element_type=F32)


def _merge(x2, attn2, rg2, gate2, w_attn_o, w_rnn_o, w_out):
    t, d = x2.shape
    tm = min(OUT_TM, t)
    rows = lambda width: pl.BlockSpec((tm, width), lambda i: (i, 0))
    const = lambda shape: pl.BlockSpec(shape, lambda i: (0, 0))
    return pl.pallas_call(
        _merge_kernel,
        grid=(t // tm,),
        in_specs=[rows(d), rows(attn2.shape[1]), rows(rg2.shape[1]), rows(2 * d),
                  const(w_attn_o.shape), const(w_rnn_o.shape), const(w_out.shape)],
        out_specs=rows(d),
        out_shape=jax.ShapeDtypeStruct((t, d), F32),
        compiler_params=_params("parallel"),
        name="merge",
    )(x2, attn2, rg2, gate2, w_attn_o, w_rnn_o, w_out)


def kernel(x, ffn1_norm, ffn1_w1, ffn1_w2, mix_norm, w_in, b_gate, q_norm, k_norm, w_attn_o, conv_w, conv_b,
           lru_wa, lru_ba, lru_wx, lru_bx, lru_lambda, w_rnn_o, w_out, ffn2_norm, ffn2_w1, ffn2_w2):
    b, s, d = x.shape
    depth = w_in.shape[0]
    x2 = x.reshape(b * s, d)
    for l in range(depth):
        x2 = _ffn(x2, ffn1_norm[l], ffn1_w1[l].astype(BF16), ffn1_w2[l].astype(BF16))
        q, k, v, xr, gy, gates = _in_proj(x2.reshape(b, s, d), mix_norm[l], w_in[l].astype(BF16), b_gate[l],
                                          q_norm[l], k_norm[l])
        attn = _attention(q, k, v)
        hf = _rnn_pass(xr, conv_w[l], conv_b[l], lru_wa[l, 0], lru_ba[l, 0], lru_wx[l, 0], lru_bx[l, 0],
                       lru_lambda[l, 0], reverse=False)
        rg = _rnn_pass(xr, conv_w[l], conv_b[l], lru_wa[l, 1], lru_ba[l, 1], lru_wx[l, 1], lru_bx[l, 1],
                       lru_lambda[l, 1], reverse=True, hf=hf, gy=gy)
        x2 = _merge(x2, attn.reshape(b * s, -1), rg.reshape(b * s, -1), gates.reshape(b * s, -1),
                    w_attn_o[l].astype(BF16), w_rnn_o[l].astype(BF16), w_out[l].astype(BF16))
        x2 = _ffn(x2, ffn2_norm[l], ffn2_w1[l].astype(BF16), ffn2_w2[l].astype(BF16))
    return x2.reshape(b, s, d)
```

```python
import functools
import math

import jax
import jax.numpy as jnp
from jax import lax
from jax.experimental import pallas as pl
from jax.experimental.pallas import tpu as pltpu

F32 = jnp.float32
BF16 = jnp.bfloat16

N_HEADS = 16
N_KV_HEADS = 4
HEAD_DIM = 64
GROUP = N_HEADS // N_KV_HEADS
VT_ROWS = HEAD_DIM + 16
SAFE_SHIFT = 60.0
ROPE_THETA = 10000.0
ROPE_QUARTER = HEAD_DIM // 4
GRID_W = 64
N_RNN_BLOCKS = 16
CONV_W = 4
CONV_LEFT = 2
LRU_C = 8.0
FFN_RESID = 0.5
EPS = 1e-6
LOG2E = math.log2(math.e)

LANES = 128
SUBLANES = 8
MXU_DIM = 256
VMEM_LIMIT_BYTES = 56 * 1024 * 1024

FFN_TM = 1024
FFN_TF = 256
PROJ_TM = 256
ATT_TQ = 256
ATT_TK = 512
ATT_UNROLL = 4
RNN_TM = 512
OUT_TM = 512


def _params(*sem):
    return pltpu.CompilerParams(dimension_semantics=sem, vmem_limit_bytes=VMEM_LIMIT_BYTES)


def _rms_scale(x):
    return lax.rsqrt(jnp.mean(x * x, axis=-1, keepdims=True) + EPS)


def _sigmoid(x):
    return 1.0 / (1.0 + jnp.exp(-x))


def _ffn_kernel(x_ref, g_ref, w1g_ref, w1u_ref, w2_ref, o_ref, h_ref, acc_ref):
    f = pl.program_id(1)

    @pl.when(f == 0)
    def _():
        x = x_ref[...]
        h_ref[...] = (x * _rms_scale(x) * g_ref[...]).astype(BF16)
        acc_ref[...] = jnp.zeros_like(acc_ref)

    h = h_ref[...]
    g = jnp.dot(h, w1g_ref[...], preferred_element_type=F32)
    u = jnp.dot(h, w1u_ref[...], preferred_element_type=F32)
    act = (g * _sigmoid(g) * u).astype(BF16)
    acc_ref[...] += jnp.dot(act, w2_ref[...], preferred_element_type=F32)

    @pl.when(f == pl.num_programs(1) - 1)
    def _():
        o_ref[...] = x_ref[...] + FFN_RESID * acc_ref[...]


def _ffn(x2, gain, w1, w2):
    t, d = x2.shape
    d_ff = w2.shape[0]
    tm = min(FFN_TM, t)
    tf = FFN_TF
    nf = d_ff // tf
    return pl.pallas_call(
        _ffn_kernel,
        grid=(t // tm, nf),
        in_specs=[
            pl.BlockSpec((tm, d), lambda i, f: (i, 0)),
            pl.BlockSpec((1, d), lambda i, f: (0, 0)),
            pl.BlockSpec((d, tf), lambda i, f: (0, f)),
            pl.BlockSpec((d, tf), lambda i, f: (0, nf + f)),
            pl.BlockSpec((tf, d), lambda i, f: (f, 0)),
        ],
        out_specs=pl.BlockSpec((tm, d), lambda i, f: (i, 0)),
        out_shape=jax.ShapeDtypeStruct((t, d), F32),
        scratch_shapes=[pltpu.VMEM((tm, d), BF16), pltpu.VMEM((tm, d), F32)],
        compiler_params=_params("parallel", "arbitrary"),
        name="ffn",
    )(x2, gain.reshape(1, d), w1, w1, w2)


def _rope_tables(seq):
    pos = jnp.arange(seq, dtype=jnp.int32)
    row = (pos // GRID_W).astype(F32)
    col = (pos % GRID_W).astype(F32)
    half = HEAD_DIM // 2
    inv = ROPE_THETA ** (-jnp.arange(0, half, 2, dtype=F32) / half)
    ang_r = row[:, None] * inv[None, :]
    ang_c = col[:, None] * inv[None, :]
    zero = jnp.zeros_like(ang_r)
    cos = jnp.concatenate([jnp.cos(ang_r), jnp.cos(ang_r), jnp.cos(ang_c), jnp.cos(ang_c)], axis=1)
    s_up = jnp.concatenate([-jnp.sin(ang_r), zero, -jnp.sin(ang_c), zero], axis=1)
    s_dn = jnp.concatenate([zero, jnp.sin(ang_r), zero, jnp.sin(ang_c)], axis=1)
    rep = LANES // HEAD_DIM
    return tuple(jnp.tile(a, (1, rep)) for a in (cos, s_up, s_dn))


def _head_norm_rope(z, seg, gain, cos, s_up, s_dn):
    width = z.shape[1]
    sq = (z * z).astype(BF16)
    ms = jnp.concatenate(
        [jnp.dot(sq[:, c:c + MXU_DIM], seg, preferred_element_type=F32) for c in range(0, width, MXU_DIM)],
        axis=1)
    zn = z * lax.rsqrt(ms + EPS) * gain
    rep = width // LANES
    widen = lambda a: jnp.concatenate([a] * rep, axis=1)
    up = pltpu.roll(zn, width - ROPE_QUARTER, axis=1)
    dn = pltpu.roll(zn, ROPE_QUARTER, axis=1)
    return zn * widen(cos) + up * widen(s_up) + dn * widen(s_dn)


def _proj_kernel(x_ref, gn_ref, wq_ref, wk_ref, wv_ref, wx_ref, wy_ref, wg_ref, bg_ref,
                 gq_ref, gk_ref, seg_ref, cos_ref, sup_ref, sdn_ref,
                 q_ref, k_ref, vt_ref, xr_ref, gy_ref, gate_ref):
    x = x_ref[0]
    h = (x * _rms_scale(x) * gn_ref[...]).astype(BF16)
    seg = seg_ref[...]
    cos, s_up, s_dn = cos_ref[...], sup_ref[...], sdn_ref[...]

    q = jnp.dot(h, wq_ref[...], preferred_element_type=F32)
    qscale = HEAD_DIM ** -0.5 * LOG2E
    q_ref[0] = (_head_norm_rope(q, seg, gq_ref[...], cos, s_up, s_dn) * qscale).astype(BF16)

    k = jnp.dot(h, wk_ref[...], preferred_element_type=F32)
    k = _head_norm_rope(k, seg, gk_ref[...], cos, s_up, s_dn).astype(BF16)
    vt = jnp.dot(h, wv_ref[...], preferred_element_type=F32).T.astype(BF16)
    for j in range(N_KV_HEADS):
        k_ref[0, j] = k[:, j * HEAD_DIM:(j + 1) * HEAD_DIM]
        vt_ref[0, j, 0, 0:HEAD_DIM, :] = vt[j * HEAD_DIM:(j + 1) * HEAD_DIM, :]
        vt_ref[0, j, 0, HEAD_DIM:VT_ROWS, :] = jnp.ones((VT_ROWS - HEAD_DIM, vt.shape[1]), BF16)

    xr_ref[0] = jnp.dot(h, wx_ref[...], preferred_element_type=F32)

    y = jnp.dot(h, wy_ref[...], preferred_element_type=F32)
    gelu = 0.5 * y * (1.0 + jnp.tanh(math.sqrt(2.0 / math.pi) * (y + 0.044715 * (y * y * y))))
    gy_ref[0] = gelu.astype(BF16)

    gl = jnp.dot(h, wg_ref[...], preferred_element_type=F32) + bg_ref[...]
    gate_ref[0] = _sigmoid(gl).astype(BF16)


def _in_proj(x, gain, w_in, b_gate, q_gain, k_gain):
    b, s, d = x.shape
    q_cols = N_HEADS * HEAD_DIM
    kv_cols = N_KV_HEADS * HEAD_DIM
    d_rnn = (w_in.shape[1] - q_cols - 2 * kv_cols - 2 * d) // 2
    o = [0, q_cols, q_cols + kv_cols, q_cols + 2 * kv_cols, q_cols + 2 * kv_cols + d_rnn,
         q_cols + 2 * kv_cols + 2 * d_rnn, w_in.shape[1]]
    ws = [w_in[:, o[i]:o[i + 1]] for i in range(6)]
    tm = min(PROJ_TM, s)
    nt = s // tm
    cos, s_up, s_dn = _rope_tables(s)
    lane = jnp.arange(MXU_DIM)
    seg = jnp.where(lane[:, None] // HEAD_DIM == lane[None, :] // HEAD_DIM, 1.0 / HEAD_DIM, 0.0).astype(BF16)
    const = lambda shape: pl.BlockSpec(shape, lambda bi, ti: (0,) * len(shape))
    tab = pl.BlockSpec((tm, LANES), lambda bi, ti: (ti, 0))
    row = lambda width: pl.BlockSpec((1, tm, width), lambda bi, ti: (bi, ti, 0))
    head = pl.BlockSpec((1, N_KV_HEADS, tm, HEAD_DIM), lambda bi, ti: (bi, 0, ti, 0))
    head_t = pl.BlockSpec((1, N_KV_HEADS, 1, VT_ROWS, tm), lambda bi, ti: (bi, 0, ti, 0, 0))
    return pl.pallas_call(
        _proj_kernel,
        grid=(b, nt),
        in_specs=[row(d), const((1, d))] + [const(w.shape) for w in ws]
        + [const((1, 2 * d)), const((1, q_cols)), const((1, kv_cols)), const((MXU_DIM, MXU_DIM)), tab, tab, tab],
        out_specs=[row(q_cols), head, head_t, row(d_rnn), row(d_rnn), row(2 * d)],
        out_shape=[
            jax.ShapeDtypeStruct((b, s, q_cols), BF16),
            jax.ShapeDtypeStruct((b, N_KV_HEADS, s, HEAD_DIM), BF16),
            jax.ShapeDtypeStruct((b, N_KV_HEADS, nt, VT_ROWS, tm), BF16),
            jax.ShapeDtypeStruct((b, s, d_rnn), F32),
            jax.ShapeDtypeStruct((b, s, d_rnn), BF16),
            jax.ShapeDtypeStruct((b, s, 2 * d), BF16),
        ],
        compiler_params=_params("parallel", "parallel"),
        name="in_proj",
    )(x, gain.reshape(1, d), *ws, b_gate.reshape(1, 2 * d),
      jnp.tile(q_gain, N_HEADS).reshape(1, q_cols), jnp.tile(k_gain, N_KV_HEADS).reshape(1, kv_cols),
      seg, cos, s_up, s_dn)


def _attn_kernel(q_ref, k_ref, vt_ref, o_ref, qt_ref, shift_ref, m_ref, acc_ref, kmax_ref, *, tk):
    tq = q_ref.shape[1]
    seq = k_ref.shape[2]
    kc = vt_ref.shape[4]
    nsub = tk // kc
    nchunk = seq // tk

    @pl.when(pl.program_id(2) == 0)
    def _():
        def norm_chunk(c, best):
            kk = k_ref[0, 0, pl.ds(pl.multiple_of(c * tk, tk), tk), :].astype(F32)
            return jnp.maximum(best, jnp.max(jnp.sum(kk * kk, axis=1, keepdims=True), axis=0, keepdims=True))
        kmax_ref[...] = jnp.sqrt(lax.fori_loop(0, nchunk, norm_chunk, jnp.zeros((1, 1), F32)))

    q_t = q_ref[0].astype(F32).T
    for g in range(GROUP):
        q_g = q_t[g * HEAD_DIM:(g + 1) * HEAD_DIM, :]
        qt_ref[:, g * tq:(g + 1) * tq] = q_g.astype(BF16)
        shift_ref[:, g * tq:(g + 1) * tq] = jnp.sqrt(jnp.sum(q_g * q_g, axis=0, keepdims=True)) * kmax_ref[...]
    acc_ref[...] = jnp.zeros_like(acc_ref)
    safe = jnp.max(shift_ref[...]) <= SAFE_SHIFT

    def pv(c, p_t):
        out = jnp.dot(vt_ref[0, 0, c * nsub], p_t[0:kc, :], preferred_element_type=F32)
        for sub in range(1, nsub):
            out += jnp.dot(vt_ref[0, 0, c * nsub + sub], p_t[sub * kc:(sub + 1) * kc, :],
                           preferred_element_type=F32)
        return out

    def scores(c):
        k = k_ref[0, 0, pl.ds(pl.multiple_of(c * tk, tk), tk), :]
        return jnp.dot(k, qt_ref[...], preferred_element_type=F32)

    @pl.when(safe)
    def _():
        def body(c, carry):
            p_t = jnp.exp2(scores(c) - shift_ref[...]).astype(BF16)
            acc_ref[...] += pv(c, p_t)
            return carry
        lax.fori_loop(0, nchunk, body, 0, unroll=min(ATT_UNROLL, nchunk))

    @pl.when(jnp.logical_not(safe))
    def _():
        m_ref[...] = jnp.full_like(m_ref, -jnp.inf)

        def body(c, carry):
            s_t = scores(c)
            m_prev = m_ref[...]
            m_new = jnp.maximum(m_prev, jnp.max(s_t, axis=0, keepdims=True))
            p_t = jnp.exp2(s_t - m_new).astype(BF16)
            acc_ref[...] = jnp.exp2(m_prev - m_new) * acc_ref[...] + pv(c, p_t)
            m_ref[...] = m_new
            return carry
        lax.fori_loop(0, nchunk, body, 0)

    out_t = acc_ref[0:HEAD_DIM, :] / acc_ref[HEAD_DIM:HEAD_DIM + 1, :]
    out = jnp.concatenate([out_t[:, g * tq:(g + 1) * tq] for g in range(GROUP)], axis=0).T
    o_ref[0] = out.astype(o_ref.dtype)


def _attention(q, k, vt):
    b, s, q_cols = q.shape
    tq = min(ATT_TQ, s)
    tk = min(ATT_TK, s)
    gw = GROUP * HEAD_DIM
    nchunk, vt_rows, kc = vt.shape[2:]
    return pl.pallas_call(
        functools.partial(_attn_kernel, tk=tk),
        grid=(b, N_KV_HEADS, s // tq),
        in_specs=[
            pl.BlockSpec((1, tq, gw), lambda bi, j, i: (bi, i, j)),
            pl.BlockSpec((1, 1, s, HEAD_DIM), lambda bi, j, i: (bi, j, 0, 0)),
            pl.BlockSpec((1, 1, nchunk, vt_rows, kc), lambda bi, j, i: (bi, j, 0, 0, 0)),
        ],
        out_specs=pl.BlockSpec((1, tq, gw), lambda bi, j, i: (bi, i, j)),
        out_shape=jax.ShapeDtypeStruct((b, s, q_cols), BF16),
        scratch_shapes=[
            pltpu.VMEM((HEAD_DIM, GROUP * tq), BF16),
            pltpu.VMEM((1, GROUP * tq), F32),
            pltpu.VMEM((1, GROUP * tq), F32),
            pltpu.VMEM((vt_rows, GROUP * tq), F32),
            pltpu.VMEM((1, 1), F32),
        ],
        compiler_params=_params("parallel", "arbitrary", "arbitrary"),
        name="attention",
    )(q, k, vt)


def _rnn_kernel(*refs, reverse):
    if reverse:
        (xr_ref, prev_ref, next_ref, cw_ref, cb_ref, wcat_ref, ba_ref, bx_ref, lam_ref, hf_ref, gy_ref,
         o_ref, xpad_ref, a_ref, u_ref, hs_ref, hc_ref) = refs
    else:
        (xr_ref, prev_ref, next_ref, cw_ref, cb_ref, wcat_ref, ba_ref, bx_ref, lam_ref,
         o_ref, xpad_ref, a_ref, u_ref, hs_ref, hc_ref) = refs
    t = pl.program_id(1)
    nt = pl.num_programs(1)
    ti = nt - 1 - t if reverse else t
    tm = xr_ref.shape[1]
    d = xr_ref.shape[2]
    halo = SUBLANES

    @pl.when(t == 0)
    def _():
        hc_ref[...] = jnp.zeros_like(hc_ref)

    xpad_ref[halo:halo + tm, :] = xr_ref[0]
    xpad_ref[0:halo, :] = jnp.where(ti > 0, prev_ref[0], 0.0)
    xpad_ref[halo + tm:halo + tm + halo, :] = jnp.where(ti < nt - 1, next_ref[0], 0.0)
    xc = cb_ref[...] + sum(
        cw_ref[kk:kk + 1, :] * xpad_ref[halo - CONV_LEFT + kk:halo - CONV_LEFT + kk + tm, :]
        for kk in range(CONV_W))

    xcb = xc.astype(BF16)
    lam = lam_ref[...]
    softplus_neg_lam = jnp.maximum(-lam, 0.0) + jnp.log1p(jnp.exp(-jnp.abs(lam)))
    for c in range(d // MXU_DIM):
        sl = slice(c * MXU_DIM, (c + 1) * MXU_DIM)
        g = jnp.dot(xcb[:, sl], wcat_ref[c], preferred_element_type=F32)
        r = _sigmoid(g[:, :MXU_DIM] + ba_ref[:, sl])
        i = _sigmoid(g[:, MXU_DIM:] + bx_ref[:, sl])
        log_a = -LRU_C * r * softplus_neg_lam[:, sl]
        a = jnp.exp(log_a)
        a_ref[:, sl] = a
        y = 1.0 - a * a
        u_ref[:, sl] = y * lax.rsqrt(jnp.maximum(y, 1e-30)) * (i * xc[:, sl])

    def group(gi, h):
        base = pl.multiple_of((tm // SUBLANES - 1 - gi if reverse else gi) * SUBLANES, SUBLANES)
        for j in range(SUBLANES):
            row = base + (SUBLANES - 1 - j if reverse else j)
            h = a_ref[pl.ds(row, 1), :] * h + u_ref[pl.ds(row, 1), :]
            hs_ref[pl.ds(row, 1), :] = h
        return h

    hc_ref[...] = lax.fori_loop(0, tm // SUBLANES, group, hc_ref[...])

    if reverse:
        o_ref[0] = ((hf_ref[0] + hs_ref[...]) * gy_ref[0].astype(F32)).astype(o_ref.dtype)
    else:
        o_ref[0] = hs_ref[...]


def _rnn_pass(xr, conv_w, conv_b, wa, ba, wx, bx, lam, reverse, hf=None, gy=None):
    b, s, d = xr.shape
    tm = min(RNN_TM, s)
    nt = s // tm
    nb, blk, _ = wa.shape
    per = MXU_DIM // blk
    nslab = d // MXU_DIM

    def slab_diag(w):
        w = w.reshape(nslab, per, blk, blk)
        eye = jnp.eye(per, dtype=w.dtype)
        return jnp.einsum('spij,pq->spiqj', w, eye).reshape(nslab, MXU_DIM, MXU_DIM)

    wcat = jnp.concatenate([slab_diag(wa), slab_diag(wx)], axis=2).astype(BF16)
    tile_of = (lambda t: nt - 1 - t) if reverse else (lambda t: t)
    hb = tm // SUBLANES
    main = lambda width: pl.BlockSpec((1, tm, width), lambda bi, t: (bi, tile_of(t), 0))
    const = lambda shape: pl.BlockSpec(shape, lambda bi, t: (0,) * len(shape))
    in_specs = [
        main(d),
        pl.BlockSpec((1, SUBLANES, d), lambda bi, t: (bi, jnp.maximum(tile_of(t) * hb - 1, 0), 0)),
        pl.BlockSpec((1, SUBLANES, d), lambda bi, t: (bi, jnp.minimum((tile_of(t) + 1) * hb, s // SUBLANES - 1), 0)),
        const((CONV_W, d)), const((1, d)), const(wcat.shape), const((1, d)), const((1, d)), const((1, d)),
    ]
    args = [xr, xr, xr, conv_w, conv_b.reshape(1, d), wcat, ba.reshape(1, d), bx.reshape(1, d), lam.reshape(1, d)]
    if reverse:
        in_specs += [main(d), main(d)]
        args += [hf, gy]
    return pl.pallas_call(
        functools.partial(_rnn_kernel, reverse=reverse),
        grid=(b, nt),
        in_specs=in_specs,
        out_specs=main(d),
        out_shape=jax.ShapeDtypeStruct((b, s, d), BF16 if reverse else F32),
        scratch_shapes=[
            pltpu.VMEM((tm + 2 * SUBLANES, d), F32),
            pltpu.VMEM((tm, d), F32),
            pltpu.VMEM((tm, d), F32),
            pltpu.VMEM((tm, d), F32),
            pltpu.VMEM((1, d), F32),
        ],
        compiler_params=_params("parallel", "arbitrary"),
        name="rnn_bwd" if reverse else "rnn_fwd",
    )(*args)


def _merge_kernel(x_ref, attn_ref, rg_ref, gate_ref, wa_ref, wr_ref, wo_ref, o_ref):
    d = x_ref.shape[1]
    a = jnp.dot(attn_ref[...], wa_ref[...], preferred_element_type=F32)
    r = jnp.dot(rg_ref[...], wr_ref[...], preferred_element_type=F32)
    merged = gate_ref[:, :d].astype(F32) * a + gate_ref[:, d:].astype(F32) * r
    o_ref[...] = x_ref[...] + jnp.dot(merged.astype(BF16), wo_ref[...], preferred_element_type=F32)


def _merge(x2, attn2, rg2, gate2, w_attn_o, w_rnn_o, w_out):
    t, d = x2.shape
    tm = min(OUT_TM, t)
    rows = lambda width: pl.BlockSpec((tm, width), lambda i: (i, 0))
    const = lambda shape: pl.BlockSpec(shape, lambda i: (0, 0))
    return pl.pallas_call(
        _merge_kernel,
        grid=(t // tm,),
        in_specs=[rows(d), rows(attn2.shape[1]), rows(rg2.shape[1]), rows(2 * d),
                  const(w_attn_o.shape), const(w_rnn_o.shape), const(w_out.shape)],
        out_specs=rows(d),
        out_shape=jax.ShapeDtypeStruct((t, d), F32),
        compiler_params=_params("parallel"),
        name="merge",
    )(x2, attn2, rg2, gate2, w_attn_o, w_rnn_o, w_out)


def kernel(x, ffn1_norm, ffn1_w1, ffn1_w2, mix_norm, w_in, b_gate, q_norm, k_norm, w_attn_o, conv_w, conv_b,
           lru_wa, lru_ba, lru_wx, lru_bx, lru_lambda, w_rnn_o, w_out, ffn2_norm, ffn2_w1, ffn2_w2):
    b, s, d = x.shape
    depth = w_in.shape[0]
    x2 = x.reshape(b * s, d)
    for l in range(depth):
        x2 = _ffn(x2, ffn1_norm[l], ffn1_w1[l].astype(BF16), ffn1_w2[l].astype(BF16))
        q, k, vt, xr, gy, gates = _in_proj(x2.reshape(b, s, d), mix_norm[l], w_in[l].astype(BF16), b_gate[l],
                                          q_norm[l], k_norm[l])
        attn = _attention(q, k, vt)
        hf = _rnn_pass(xr, conv_w[l], conv_b[l], lru_wa[l, 0], lru_ba[l, 0], lru_wx[l, 0], lru_bx[l, 0],
                       lru_lambda[l, 0], reverse=False)
        rg = _rnn_pass(xr, conv_w[l], conv_b[l], lru_wa[l, 1], lru_ba[l, 1], lru_wx[l, 1], lru_bx[l, 1],
                       lru_lambda[l, 1], reverse=True, hf=hf, gy=gy)
        x2 = _merge(x2, attn.reshape(b * s, -1), rg.reshape(b * s, -1), gates.reshape(b * s, -1),
                    w_attn_o[l].astype(BF16), w_rnn_o[l].astype(BF16), w_out[l].astype(BF16))
        x2 = _ffn(x2, ffn2_norm[l], ffn2_w1[l].astype(BF16), ffn2_w2[l].astype(BF16))
    return x2.reshape(b, s, d)
```

```python
import functools
import math

import jax
import jax.numpy as jnp
from jax import lax
from jax.experimental import pallas as pl
from jax.experimental.pallas import tpu as pltpu

F32 = jnp.float32
BF16 = jnp.bfloat16

N_HEADS = 16
N_KV_HEADS = 4
HEAD_DIM = 64
GROUP = N_HEADS // N_KV_HEADS
VT_ROWS = HEAD_DIM + 16
SAFE_SHIFT = 60.0
ROPE_THETA = 10000.0
ROPE_QUARTER = HEAD_DIM // 4
GRID_W = 64
N_RNN_BLOCKS = 16
CONV_W = 4
CONV_LEFT = 2
LRU_C = 8.0
FFN_RESID = 0.5
EPS = 1e-6
LOG2E = math.log2(math.e)

LANES = 128
SUBLANES = 8
MXU_DIM = 256
VMEM_LIMIT_BYTES = 56 * 1024 * 1024

FFN_TM = 2048
FFN_TF = 256
PROJ_TM = 512
ATT_TQ = 256
ATT_TK = 1024
ATT_UNROLL = 2
RNN_TM = 512
OUT_TM = 512


def _params(*sem):
    return pltpu.CompilerParams(dimension_semantics=sem, vmem_limit_bytes=VMEM_LIMIT_BYTES)


def _rms_scale(x):
    return lax.rsqrt(jnp.mean(x * x, axis=-1, keepdims=True) + EPS)


def _sigmoid(x):
    return 1.0 / (1.0 + jnp.exp(-x))


def _ffn_kernel(x_ref, g_ref, w1g_ref, w1u_ref, w2_ref, o_ref, h_ref, acc_ref):
    f = pl.program_id(1)

    @pl.when(f == 0)
    def _():
        x = x_ref[...]
        h_ref[...] = (x * _rms_scale(x) * g_ref[...]).astype(BF16)
        acc_ref[...] = jnp.zeros_like(acc_ref)

    h = h_ref[...]
    g = jnp.dot(h, w1g_ref[...], preferred_element_type=F32)
    u = jnp.dot(h, w1u_ref[...], preferred_element_type=F32)
    act = (g * _sigmoid(g) * u).astype(BF16)
    acc_ref[...] += jnp.dot(act, w2_ref[...], preferred_element_type=F32)

    @pl.when(f == pl.num_programs(1) - 1)
    def _():
        o_ref[...] = x_ref[...] + FFN_RESID * acc_ref[...]


def _ffn(x2, gain, w1, w2):
    t, d = x2.shape
    d_ff = w2.shape[0]
    tm = min(FFN_TM, t)
    tf = FFN_TF
    nf = d_ff // tf
    return pl.pallas_call(
        _ffn_kernel,
        grid=(t // tm, nf),
        in_specs=[
            pl.BlockSpec((tm, d), lambda i, f: (i, 0)),
            pl.BlockSpec((1, d), lambda i, f: (0, 0)),
            pl.BlockSpec((d, tf), lambda i, f: (0, f)),
            pl.BlockSpec((d, tf), lambda i, f: (0, nf + f)),
            pl.BlockSpec((tf, d), lambda i, f: (f, 0)),
        ],
        out_specs=pl.BlockSpec((tm, d), lambda i, f: (i, 0)),
        out_shape=jax.ShapeDtypeStruct((t, d), F32),
        scratch_shapes=[pltpu.VMEM((tm, d), BF16), pltpu.VMEM((tm, d), F32)],
        compiler_params=_params("parallel", "arbitrary"),
        name="ffn",
    )(x2, gain.reshape(1, d), w1, w1, w2)


def _rope_tables(seq):
    pos = jnp.arange(seq, dtype=jnp.int32)
    row = (pos // GRID_W).astype(F32)
    col = (pos % GRID_W).astype(F32)
    half = HEAD_DIM // 2
    inv = ROPE_THETA ** (-jnp.arange(0, half, 2, dtype=F32) / half)
    ang_r = row[:, None] * inv[None, :]
    ang_c = col[:, None] * inv[None, :]
    zero = jnp.zeros_like(ang_r)
    cos = jnp.concatenate([jnp.cos(ang_r), jnp.cos(ang_r), jnp.cos(ang_c), jnp.cos(ang_c)], axis=1)
    s_up = jnp.concatenate([-jnp.sin(ang_r), zero, -jnp.sin(ang_c), zero], axis=1)
    s_dn = jnp.concatenate([zero, jnp.sin(ang_r), zero, jnp.sin(ang_c)], axis=1)
    rep = LANES // HEAD_DIM
    return tuple(jnp.tile(a, (1, rep)) for a in (cos, s_up, s_dn))


def _head_norm_rope(z, seg, gain, cos, s_up, s_dn):
    width = z.shape[1]
    sq = (z * z).astype(BF16)
    ms = jnp.concatenate(
        [jnp.dot(sq[:, c:c + MXU_DIM], seg, preferred_element_type=F32) for c in range(0, width, MXU_DIM)],
        axis=1)
    zn = z * lax.rsqrt(ms + EPS) * gain
    rep = width // LANES
    widen = lambda a: jnp.concatenate([a] * rep, axis=1)
    up = pltpu.roll(zn, width - ROPE_QUARTER, axis=1)
    dn = pltpu.roll(zn, ROPE_QUARTER, axis=1)
    return zn * widen(cos) + up * widen(s_up) + dn * widen(s_dn)


def _proj_kernel(x_ref, gn_ref, wq_ref, wk_ref, wv_ref, wx_ref, wy_ref, wg_ref, bg_ref,
                 gq_ref, gk_ref, seg_ref, cos_ref, sup_ref, sdn_ref,
                 q_ref, k_ref, vt_ref, xr_ref, gy_ref, gate_ref):
    x = x_ref[0]
    h = (x * _rms_scale(x) * gn_ref[...]).astype(BF16)
    seg = seg_ref[...]
    cos, s_up, s_dn = cos_ref[...], sup_ref[...], sdn_ref[...]

    q = jnp.dot(h, wq_ref[...], preferred_element_type=F32)
    qscale = HEAD_DIM ** -0.5 * LOG2E
    q_ref[0] = (_head_norm_rope(q, seg, gq_ref[...], cos, s_up, s_dn) * qscale).astype(BF16)

    k = jnp.dot(h, wk_ref[...], preferred_element_type=F32)
    k = _head_norm_rope(k, seg, gk_ref[...], cos, s_up, s_dn).astype(BF16)
    vt = jnp.dot(h, wv_ref[...], preferred_element_type=F32).T.astype(BF16)
    for j in range(N_KV_HEADS):
        k_ref[0, j] = k[:, j * HEAD_DIM:(j + 1) * HEAD_DIM]
        vt_ref[0, j, 0, 0:HEAD_DIM, :] = vt[j * HEAD_DIM:(j + 1) * HEAD_DIM, :]
        vt_ref[0, j, 0, HEAD_DIM:VT_ROWS, :] = jnp.ones((VT_ROWS - HEAD_DIM, vt.shape[1]), BF16)

    xr_ref[0] = jnp.dot(h, wx_ref[...], preferred_element_type=F32)

    y = jnp.dot(h, wy_ref[...], preferred_element_type=F32)
    gelu = 0.5 * y * (1.0 + jnp.tanh(math.sqrt(2.0 / math.pi) * (y + 0.044715 * (y * y * y))))
    gy_ref[0] = gelu.astype(BF16)

    gl = jnp.dot(h, wg_ref[...], preferred_element_type=F32) + bg_ref[...]
    gate_ref[0] = _sigmoid(gl).astype(BF16)


def _in_proj(x, gain, w_in, b_gate, q_gain, k_gain):
    b, s, d = x.shape
    q_cols = N_HEADS * HEAD_DIM
    kv_cols = N_KV_HEADS * HEAD_DIM
    d_rnn = (w_in.shape[1] - q_cols - 2 * kv_cols - 2 * d) // 2
    o = [0, q_cols, q_cols + kv_cols, q_cols + 2 * kv_cols, q_cols + 2 * kv_cols + d_rnn,
         q_cols + 2 * kv_cols + 2 * d_rnn, w_in.shape[1]]
    ws = [w_in[:, o[i]:o[i + 1]] for i in range(6)]
    tm = min(PROJ_TM, s)
    nt = s // tm
    cos, s_up, s_dn = _rope_tables(s)
    lane = jnp.arange(MXU_DIM)
    seg = jnp.where(lane[:, None] // HEAD_DIM == lane[None, :] // HEAD_DIM, 1.0 / HEAD_DIM, 0.0).astype(BF16)
    const = lambda shape: pl.BlockSpec(shape, lambda bi, ti: (0,) * len(shape))
    tab = pl.BlockSpec((tm, LANES), lambda bi, ti: (ti, 0))
    row = lambda width: pl.BlockSpec((1, tm, width), lambda bi, ti: (bi, ti, 0))
    head = pl.BlockSpec((1, N_KV_HEADS, tm, HEAD_DIM), lambda bi, ti: (bi, 0, ti, 0))
    head_t = pl.BlockSpec((1, N_KV_HEADS, 1, VT_ROWS, tm), lambda bi, ti: (bi, 0, ti, 0, 0))
    return pl.pallas_call(
        _proj_kernel,
        grid=(b, nt),
        in_specs=[row(d), const((1, d))] + [const(w.shape) for w in ws]
        + [const((1, 2 * d)), const((1, q_cols)), const((1, kv_cols)), const((MXU_DIM, MXU_DIM)), tab, tab, tab],
        out_specs=[row(q_cols), head, head_t, row(d_rnn), row(d_rnn), row(2 * d)],
        out_shape=[
            jax.ShapeDtypeStruct((b, s, q_cols), BF16),
            jax.ShapeDtypeStruct((b, N_KV_HEADS, s, HEAD_DIM), BF16),
            jax.ShapeDtypeStruct((b, N_KV_HEADS, nt, VT_ROWS, tm), BF16),
            jax.ShapeDtypeStruct((b, s, d_rnn), F32),
            jax.ShapeDtypeStruct((b, s, d_rnn), BF16),
            jax.ShapeDtypeStruct((b, s, 2 * d), BF16),
        ],
        compiler_params=_params("parallel", "parallel"),
        name="in_proj",
    )(x, gain.reshape(1, d), *ws, b_gate.reshape(1, 2 * d),
      jnp.tile(q_gain, N_HEADS).reshape(1, q_cols), jnp.tile(k_gain, N_KV_HEADS).reshape(1, kv_cols),
      seg, cos, s_up, s_dn)


def _attn_kernel(q_ref, k_ref, vt_ref, o_ref, qt_ref, shift_ref, m_ref, acc_ref, kmax_ref, *, tk):
    tq = q_ref.shape[1]
    seq = k_ref.shape[2]
    kc = vt_ref.shape[4]
    nsub = tk // kc
    nchunk = seq // tk

    @pl.when(pl.program_id(2) == 0)
    def _():
        def norm_chunk(c, best):
            kk = k_ref[0, 0, pl.ds(pl.multiple_of(c * tk, tk), tk), :].astype(F32)
            return jnp.maximum(best, jnp.max(jnp.sum(kk * kk, axis=1, keepdims=True), axis=0, keepdims=True))
        kmax_ref[...] = jnp.sqrt(lax.fori_loop(0, nchunk, norm_chunk, jnp.zeros((1, 1), F32)))

    q_t = q_ref[0].astype(F32).T
    for g in range(GROUP):
        q_g = q_t[g * HEAD_DIM:(g + 1) * HEAD_DIM, :]
        qt_ref[:, g * tq:(g + 1) * tq] = q_g.astype(BF16)
        shift_ref[:, g * tq:(g + 1) * tq] = jnp.sqrt(jnp.sum(q_g * q_g, axis=0, keepdims=True)) * kmax_ref[...]
    acc_ref[...] = jnp.zeros_like(acc_ref)
    safe = jnp.max(shift_ref[...]) <= SAFE_SHIFT

    def pv(c, p_t):
        vt = jnp.concatenate([vt_ref[0, 0, c * nsub + sub] for sub in range(nsub)], axis=1)
        return jnp.dot(vt, p_t, preferred_element_type=F32)

    def scores(c):
        k = k_ref[0, 0, pl.ds(pl.multiple_of(c * tk, tk), tk), :]
        return jnp.dot(k, qt_ref[...], preferred_element_type=F32)

    @pl.when(safe)
    def _():
        def body(c, carry):
            p_t = jnp.exp2(scores(c) - shift_ref[...]).astype(BF16)
            acc_ref[...] += pv(c, p_t)
            return carry
        lax.fori_loop(0, nchunk, body, 0, unroll=min(ATT_UNROLL, nchunk))

    @pl.when(jnp.logical_not(safe))
    def _():
        m_ref[...] = jnp.full_like(m_ref, -jnp.inf)

        def body(c, carry):
            s_t = scores(c)
            m_prev = m_ref[...]
            m_new = jnp.maximum(m_prev, jnp.max(s_t, axis=0, keepdims=True))
            p_t = jnp.exp2(s_t - m_new).astype(BF16)
            acc_ref[...] = jnp.exp2(m_prev - m_new) * acc_ref[...] + pv(c, p_t)
            m_ref[...] = m_new
            return carry
        lax.fori_loop(0, nchunk, body, 0)

    out_t = acc_ref[0:HEAD_DIM, :] / acc_ref[HEAD_DIM:HEAD_DIM + 1, :]
    out = jnp.concatenate([out_t[:, g * tq:(g + 1) * tq] for g in range(GROUP)], axis=0).T
    o_ref[0] = out.astype(o_ref.dtype)


def _attention(q, k, vt):
    b, s, q_cols = q.shape
    tq = min(ATT_TQ, s)
    tk = min(ATT_TK, s)
    gw = GROUP * HEAD_DIM
    nchunk, vt_rows, kc = vt.shape[2:]
    return pl.pallas_call(
        functools.partial(_attn_kernel, tk=tk),
        grid=(b, N_KV_HEADS, s // tq),
        in_specs=[
            pl.BlockSpec((1, tq, gw), lambda bi, j, i: (bi, i, j)),
            pl.BlockSpec((1, 1, s, HEAD_DIM), lambda bi, j, i: (bi, j, 0, 0)),
            pl.BlockSpec((1, 1, nchunk, vt_rows, kc), lambda bi, j, i: (bi, j, 0, 0, 0)),
        ],
        out_specs=pl.BlockSpec((1, tq, gw), lambda bi, j, i: (bi, i, j)),
        out_shape=jax.ShapeDtypeStruct((b, s, q_cols), BF16),
        scratch_shapes=[
            pltpu.VMEM((HEAD_DIM, GROUP * tq), BF16),
            pltpu.VMEM((1, GROUP * tq), F32),
            pltpu.VMEM((1, GROUP * tq), F32),
            pltpu.VMEM((vt_rows, GROUP * tq), F32),
            pltpu.VMEM((1, 1), F32),
        ],
        compiler_params=_params("parallel", "arbitrary", "arbitrary"),
        name="attention",
    )(q, k, vt)


def _lru_gates_and_scan(xc, wcat_ref, ba_ref, bx_ref, lam_ref, a_ref, u_ref, hs_ref, hc_ref, reverse):
    tm, d = xc.shape
    xcb = xc.astype(BF16)
    lam = lam_ref[...]
    softplus_neg_lam = jnp.maximum(-lam, 0.0) + jnp.log1p(jnp.exp(-jnp.abs(lam)))
    for c in range(d // MXU_DIM):
        sl = slice(c * MXU_DIM, (c + 1) * MXU_DIM)
        g = jnp.dot(xcb[:, sl], wcat_ref[c], preferred_element_type=F32)
        r = _sigmoid(g[:, :MXU_DIM] + ba_ref[:, sl])
        i = _sigmoid(g[:, MXU_DIM:] + bx_ref[:, sl])
        log_a = -LRU_C * r * softplus_neg_lam[:, sl]
        a = jnp.exp(log_a)
        a_ref[:, sl] = a
        y = 1.0 - a * a
        u_ref[:, sl] = y * lax.rsqrt(jnp.maximum(y, 1e-30)) * (i * xc[:, sl])

    def group(gi, h):
        base = pl.multiple_of((tm // SUBLANES - 1 - gi if reverse else gi) * SUBLANES, SUBLANES)
        for j in range(SUBLANES):
            row = base + (SUBLANES - 1 - j if reverse else j)
            h = a_ref[pl.ds(row, 1), :] * h + u_ref[pl.ds(row, 1), :]
            hs_ref[pl.ds(row, 1), :] = h
        return h

    hc_ref[...] = lax.fori_loop(0, tm // SUBLANES, group, hc_ref[...])


def _rnn_fwd_kernel(xr_ref, prev_ref, next_ref, cw_ref, cb_ref, wcat_ref, ba_ref, bx_ref, lam_ref,
                    hf_ref, xc_ref, xpad_ref, a_ref, u_ref, hc_ref):
    t = pl.program_id(1)
    nt = pl.num_programs(1)
    tm = xr_ref.shape[1]
    halo = SUBLANES

    @pl.when(t == 0)
    def _():
        hc_ref[...] = jnp.zeros_like(hc_ref)

    xpad_ref[halo:halo + tm, :] = xr_ref[0]
    xpad_ref[0:halo, :] = jnp.where(t > 0, prev_ref[0], 0.0)
    xpad_ref[halo + tm:halo + tm + halo, :] = jnp.where(t < nt - 1, next_ref[0], 0.0)
    xc = cb_ref[...] + sum(
        cw_ref[kk:kk + 1, :] * xpad_ref[halo - CONV_LEFT + kk:halo - CONV_LEFT + kk + tm, :]
        for kk in range(CONV_W))
    xc_ref[0] = xc
    _lru_gates_and_scan(xc, wcat_ref, ba_ref, bx_ref, lam_ref, a_ref, u_ref, hf_ref.at[0], hc_ref, reverse=False)


def _rnn_bwd_kernel(xc_ref, wcat_ref, ba_ref, bx_ref, lam_ref, hf_ref, gy_ref, o_ref, a_ref, u_ref, hs_ref, hc_ref):
    @pl.when(pl.program_id(1) == 0)
    def _():
        hc_ref[...] = jnp.zeros_like(hc_ref)

    _lru_gates_and_scan(xc_ref[0], wcat_ref, ba_ref, bx_ref, lam_ref, a_ref, u_ref, hs_ref, hc_ref, reverse=True)
    o_ref[0] = ((hf_ref[0] + hs_ref[...]) * gy_ref[0].astype(F32)).astype(o_ref.dtype)


def _gate_slabs(wa, wx, d):
    nb, blk, _ = wa.shape
    per = MXU_DIM // blk
    nslab = d // MXU_DIM

    def slab_diag(w):
        w = w.reshape(nslab, per, blk, blk)
        eye = jnp.eye(per, dtype=w.dtype)
        return jnp.einsum('spij,pq->spiqj', w, eye).reshape(nslab, MXU_DIM, MXU_DIM)

    return jnp.concatenate([slab_diag(wa), slab_diag(wx)], axis=2).astype(BF16)


def _rnn_fwd(xr, conv_w, conv_b, wa, ba, wx, bx, lam):
    b, s, d = xr.shape
    tm = min(RNN_TM, s)
    nt = s // tm
    hb = tm // SUBLANES
    wcat = _gate_slabs(wa, wx, d)
    main = pl.BlockSpec((1, tm, d), lambda bi, t: (bi, t, 0))
    const = lambda shape: pl.BlockSpec(shape, lambda bi, t: (0,) * len(shape))
    return pl.pallas_call(
        _rnn_fwd_kernel,
        grid=(b, nt),
        in_specs=[
            main,
            pl.BlockSpec((1, SUBLANES, d), lambda bi, t: (bi, jnp.maximum(t * hb - 1, 0), 0)),
            pl.BlockSpec((1, SUBLANES, d), lambda bi, t: (bi, jnp.minimum((t + 1) * hb, s // SUBLANES - 1), 0)),
            const((CONV_W, d)), const((1, d)), const(wcat.shape), const((1, d)), const((1, d)), const((1, d)),
        ],
        out_specs=[main, main],
        out_shape=[jax.ShapeDtypeStruct((b, s, d), F32), jax.ShapeDtypeStruct((b, s, d), F32)],
        scratch_shapes=[
            pltpu.VMEM((tm + 2 * SUBLANES, d), F32),
            pltpu.VMEM((tm, d), F32),
            pltpu.VMEM((tm, d), F32),
            pltpu.VMEM((1, d), F32),
        ],
        compiler_params=_params("parallel", "arbitrary"),
        name="rnn_fwd",
    )(xr, xr, xr, conv_w, conv_b.reshape(1, d), wcat, ba.reshape(1, d), bx.reshape(1, d), lam.reshape(1, d))


def _rnn_bwd(xc, wa, ba, wx, bx, lam, hf, gy):
    b, s, d = xc.shape
    tm = min(RNN_TM, s)
    nt = s // tm
    wcat = _gate_slabs(wa, wx, d)
    main = pl.BlockSpec((1, tm, d), lambda bi, t: (bi, nt - 1 - t, 0))
    const = lambda shape: pl.BlockSpec(shape, lambda bi, t: (0,) * len(shape))
    return pl.pallas_call(
        _rnn_bwd_kernel,
        grid=(b, nt),
        in_specs=[main, const(wcat.shape), const((1, d)), const((1, d)), const((1, d)), main, main],
        out_specs=main,
        out_shape=jax.ShapeDtypeStruct((b, s, d), BF16),
        scratch_shapes=[
            pltpu.VMEM((tm, d), F32),
            pltpu.VMEM((tm, d), F32),
            pltpu.VMEM((tm, d), F32),
            pltpu.VMEM((1, d), F32),
        ],
        compiler_params=_params("parallel", "arbitrary"),
        name="rnn_bwd",
    )(xc, wcat, ba.reshape(1, d), bx.reshape(1, d), lam.reshape(1, d), hf, gy)


def _merge_kernel(x_ref, attn_ref, rg_ref, gate_ref, wa_ref, wr_ref, wo_ref, o_ref):
    d = x_ref.shape[1]
    a = jnp.dot(attn_ref[...], wa_ref[...], preferred_element_type=F32)
    r = jnp.dot(rg_ref[...], wr_ref[...], preferred_element_type=F32)
    merged = gate_ref[:, :d].astype(F32) * a + gate_ref[:, d:].astype(F32) * r
    o_ref[...] = x_ref[...] + jnp.dot(merged.astype(BF16), wo_ref[...], preferred_element_type=F32)


def _merge(x2, attn2, rg2, gate2, w_attn_o, w_rnn_o, w_out):
    t, d = x2.shape
    tm = min(OUT_TM, t)
    rows = lambda width: pl.BlockSpec((tm, width), lambda i: (i, 0))
    const = lambda shape: pl.BlockSpec(shape, lambda i: (0, 0))
    return pl.pallas_call(
        _merge_kernel,
        grid=(t // tm,),
        in_specs=[rows(d), rows(attn2.shape[1]), rows(rg2.shape[1]), rows(2 * d),
                  const(w_attn_o.shape), const(w_rnn_o.shape), const(w_out.shape)],
        out_specs=rows(d),
        out_shape=jax.ShapeDtypeStruct((t, d), F32),
        compiler_params=_params("parallel"),
        name="merge",
    )(x2, attn2, rg2, gate2, w_attn_o, w_rnn_o, w_out)


def kernel(x, ffn1_norm, ffn1_w1, ffn1_w2, mix_norm, w_in, b_gate, q_norm, k_norm, w_attn_o, conv_w, conv_b,
           lru_wa, lru_ba, lru_wx, lru_bx, lru_lambda, w_rnn_o, w_out, ffn2_norm, ffn2_w1, ffn2_w2):
    b, s, d = x.shape
    depth = w_in.shape[0]
    x2 = x.reshape(b * s, d)
    for l in range(depth):
        x2 = _ffn(x2, ffn1_norm[l], ffn1_w1[l].astype(BF16), ffn1_w2[l].astype(BF16))
        q, k, vt, xr, gy, gates = _in_proj(x2.reshape(b, s, d), mix_norm[l], w_in[l].astype(BF16), b_gate[l],
                                          q_norm[l], k_norm[l])
        attn = _attention(q, k, vt)
        hf, xc = _rnn_fwd(xr, conv_w[l], conv_b[l], lru_wa[l, 0], lru_ba[l, 0], lru_wx[l, 0], lru_bx[l, 0],
                          lru_lambda[l, 0])
        rg = _rnn_bwd(xc, lru_wa[l, 1], lru_ba[l, 1], lru_wx[l, 1], lru_bx[l, 1], lru_lambda[l, 1], hf, gy)
        x2 = _merge(x2, attn.reshape(b * s, -1), rg.reshape(b * s, -1), gates.reshape(b * s, -1),
                    w_attn_o[l].astype(BF16), w_rnn_o[l].astype(BF16), w_out[l].astype(BF16))
        x2 = _ffn(x2, ffn2_norm[l], ffn2_w1[l].astype(BF16), ffn2_w2[l].astype(BF16))
    return x2.reshape(b, s, d)
```

```python
import functools
import math

import jax
import jax.numpy as jnp
from jax import lax
from jax.experimental import pallas as pl
from jax.experimental.pallas import tpu as pltpu

F32 = jnp.float32
BF16 = jnp.bfloat16

N_HEADS = 16
N_KV_HEADS = 4
HEAD_DIM = 64
GROUP = N_HEADS // N_KV_HEADS
VT_ROWS = HEAD_DIM + 16
SAFE_SHIFT = 60.0
ROPE_THETA = 10000.0
ROPE_QUARTER = HEAD_DIM // 4
GRID_W = 64
N_RNN_BLOCKS = 16
CONV_W = 4
CONV_LEFT = 2
LRU_C = 8.0
FFN_RESID = 0.5
EPS = 1e-6
LOG2E = math.log2(math.e)

LANES = 128
SUBLANES = 8
MXU_DIM = 256
VMEM_LIMIT_BYTES = 56 * 1024 * 1024

FFN_TM = 2048
FFN_TF = 256
PROJ_TM = 512
ATT_TQ = 512
ATT_TK = 512
ATT_UNROLL = 2
RNN_TM = 512
RNN_ROW_BLOCK = 256
OUT_TM = 512


def _params(*sem):
    return pltpu.CompilerParams(dimension_semantics=sem, vmem_limit_bytes=VMEM_LIMIT_BYTES)


def _rms_scale(x):
    return lax.rsqrt(jnp.mean(x * x, axis=-1, keepdims=True) + EPS)


def _sigmoid(x):
    return 1.0 / (1.0 + jnp.exp2(x * -LOG2E))


def _ffn_kernel(x_ref, g_ref, w1g_ref, w1u_ref, w2_ref, o_ref, h_ref, acc_ref):
    f = pl.program_id(1)

    @pl.when(f == 0)
    def _():
        x = x_ref[...]
        h_ref[...] = (x * _rms_scale(x) * g_ref[...]).astype(BF16)
        acc_ref[...] = jnp.zeros_like(acc_ref)

    h = h_ref[...]
    g = jnp.dot(h, w1g_ref[...], preferred_element_type=F32)
    u = jnp.dot(h, w1u_ref[...], preferred_element_type=F32)
    act = (g * _sigmoid(g) * u).astype(BF16)
    acc_ref[...] += jnp.dot(act, w2_ref[...], preferred_element_type=F32)

    @pl.when(f == pl.num_programs(1) - 1)
    def _():
        o_ref[...] = x_ref[...] + FFN_RESID * acc_ref[...]


def _ffn(x2, gain, w1, w2):
    t, d = x2.shape
    d_ff = w2.shape[0]
    tm = min(FFN_TM, t)
    tf = FFN_TF
    nf = d_ff // tf
    return pl.pallas_call(
        _ffn_kernel,
        grid=(t // tm, nf),
        in_specs=[
            pl.BlockSpec((tm, d), lambda i, f: (i, 0)),
            pl.BlockSpec((1, d), lambda i, f: (0, 0)),
            pl.BlockSpec((d, tf), lambda i, f: (0, f)),
            pl.BlockSpec((d, tf), lambda i, f: (0, nf + f)),
            pl.BlockSpec((tf, d), lambda i, f: (f, 0)),
        ],
        out_specs=pl.BlockSpec((tm, d), lambda i, f: (i, 0)),
        out_shape=jax.ShapeDtypeStruct((t, d), F32),
        scratch_shapes=[pltpu.VMEM((tm, d), BF16), pltpu.VMEM((tm, d), F32)],
        compiler_params=_params("parallel", "arbitrary"),
        name="ffn",
    )(x2, gain.reshape(1, d), w1, w1, w2)


def _rope_tables(seq):
    pos = jnp.arange(seq, dtype=jnp.int32)
    row = (pos // GRID_W).astype(F32)
    col = (pos % GRID_W).astype(F32)
    half = HEAD_DIM // 2
    inv = ROPE_THETA ** (-jnp.arange(0, half, 2, dtype=F32) / half)
    ang_r = row[:, None] * inv[None, :]
    ang_c = col[:, None] * inv[None, :]
    zero = jnp.zeros_like(ang_r)
    cos = jnp.concatenate([jnp.cos(ang_r), jnp.cos(ang_r), jnp.cos(ang_c), jnp.cos(ang_c)], axis=1)
    s_up = jnp.concatenate([-jnp.sin(ang_r), zero, -jnp.sin(ang_c), zero], axis=1)
    s_dn = jnp.concatenate([zero, jnp.sin(ang_r), zero, jnp.sin(ang_c)], axis=1)
    rep = LANES // HEAD_DIM
    return tuple(jnp.tile(a, (1, rep)) for a in (cos, s_up, s_dn))


def _head_norm_rope(z, seg, gain, cos, s_up, s_dn):
    width = z.shape[1]
    sq = (z * z).astype(BF16)
    ms = jnp.concatenate(
        [jnp.dot(sq[:, c:c + MXU_DIM], seg, preferred_element_type=F32) for c in range(0, width, MXU_DIM)],
        axis=1)
    zn = z * lax.rsqrt(ms + EPS) * gain
    rep = width // LANES
    widen = lambda a: jnp.concatenate([a] * rep, axis=1)
    up = pltpu.roll(zn, width - ROPE_QUARTER, axis=1)
    dn = pltpu.roll(zn, ROPE_QUARTER, axis=1)
    return zn * widen(cos) + up * widen(s_up) + dn * widen(s_dn)


def _proj_kernel(x_ref, gn_ref, wq_ref, wk_ref, wv_ref, wx_ref, wy_ref, wg_ref, bg_ref,
                 gq_ref, gk_ref, seg_ref, cos_ref, sup_ref, sdn_ref,
                 q_ref, k_ref, vt_ref, xr_ref, gy_ref, gate_ref):
    x = x_ref[0]
    h = (x * _rms_scale(x) * gn_ref[...]).astype(BF16)
    seg = seg_ref[...]
    cos, s_up, s_dn = cos_ref[...], sup_ref[...], sdn_ref[...]

    q = jnp.dot(h, wq_ref[...], preferred_element_type=F32)
    qscale = HEAD_DIM ** -0.5 * LOG2E
    q_ref[0] = (_head_norm_rope(q, seg, gq_ref[...], cos, s_up, s_dn) * qscale).astype(BF16)

    k = jnp.dot(h, wk_ref[...], preferred_element_type=F32)
    k = _head_norm_rope(k, seg, gk_ref[...], cos, s_up, s_dn).astype(BF16)
    vt = jnp.dot(h, wv_ref[...], preferred_element_type=F32).T.astype(BF16)
    for j in range(N_KV_HEADS):
        k_ref[0, j] = k[:, j * HEAD_DIM:(j + 1) * HEAD_DIM]
        vt_ref[0, j, 0, 0:HEAD_DIM, :] = vt[j * HEAD_DIM:(j + 1) * HEAD_DIM, :]
        vt_ref[0, j, 0, HEAD_DIM:VT_ROWS, :] = jnp.ones((VT_ROWS - HEAD_DIM, vt.shape[1]), BF16)

    xr_ref[0] = jnp.dot(h, wx_ref[...], preferred_element_type=F32)

    y = jnp.dot(h, wy_ref[...], preferred_element_type=F32)
    gelu = 0.5 * y * (1.0 + jnp.tanh(math.sqrt(2.0 / math.pi) * (y + 0.044715 * (y * y * y))))
    gy_ref[0] = gelu.astype(BF16)

    gl = jnp.dot(h, wg_ref[...], preferred_element_type=F32) + bg_ref[...]
    gate_ref[0] = _sigmoid(gl).astype(BF16)


def _in_proj(x, gain, w_in, b_gate, q_gain, k_gain):
    b, s, d = x.shape
    q_cols = N_HEADS * HEAD_DIM
    kv_cols = N_KV_HEADS * HEAD_DIM
    d_rnn = (w_in.shape[1] - q_cols - 2 * kv_cols - 2 * d) // 2
    o = [0, q_cols, q_cols + kv_cols, q_cols + 2 * kv_cols, q_cols + 2 * kv_cols + d_rnn,
         q_cols + 2 * kv_cols + 2 * d_rnn, w_in.shape[1]]
    ws = [w_in[:, o[i]:o[i + 1]] for i in range(6)]
    tm = min(PROJ_TM, s)
    nt = s // tm
    cos, s_up, s_dn = _rope_tables(s)
    lane = jnp.arange(MXU_DIM)
    seg = jnp.where(lane[:, None] // HEAD_DIM == lane[None, :] // HEAD_DIM, 1.0 / HEAD_DIM, 0.0).astype(BF16)
    const = lambda shape: pl.BlockSpec(shape, lambda bi, ti: (0,) * len(shape))
    tab = pl.BlockSpec((tm, LANES), lambda bi, ti: (ti, 0))
    row = lambda width: pl.BlockSpec((1, tm, width), lambda bi, ti: (bi, ti, 0))
    head = pl.BlockSpec((1, N_KV_HEADS, tm, HEAD_DIM), lambda bi, ti: (bi, 0, ti, 0))
    head_t = pl.BlockSpec((1, N_KV_HEADS, 1, VT_ROWS, tm), lambda bi, ti: (bi, 0, ti, 0, 0))
    return pl.pallas_call(
        _proj_kernel,
        grid=(b, nt),
        in_specs=[row(d), const((1, d))] + [const(w.shape) for w in ws]
        + [const((1, 2 * d)), const((1, q_cols)), const((1, kv_cols)), const((MXU_DIM, MXU_DIM)), tab, tab, tab],
        out_specs=[row(q_cols), head, head_t, row(d_rnn), row(d_rnn), row(2 * d)],
        out_shape=[
            jax.ShapeDtypeStruct((b, s, q_cols), BF16),
            jax.ShapeDtypeStruct((b, N_KV_HEADS, s, HEAD_DIM), BF16),
            jax.ShapeDtypeStruct((b, N_KV_HEADS, nt, VT_ROWS, tm), BF16),
            jax.ShapeDtypeStruct((b, s, d_rnn), F32),
            jax.ShapeDtypeStruct((b, s, d_rnn), BF16),
            jax.ShapeDtypeStruct((b, s, 2 * d), BF16),
        ],
        compiler_params=_params("parallel", "parallel"),
        name="in_proj",
    )(x, gain.reshape(1, d), *ws, b_gate.reshape(1, 2 * d),
      jnp.tile(q_gain, N_HEADS).reshape(1, q_cols), jnp.tile(k_gain, N_KV_HEADS).reshape(1, kv_cols),
      seg, cos, s_up, s_dn)


def _attn_kernel(q_ref, k_ref, vt_ref, o_ref, qt_ref, shift_ref, m_ref, acc_ref, kmax_ref, *, tk):
    tq = q_ref.shape[1]
    seq = k_ref.shape[2]
    kc = vt_ref.shape[4]
    nsub = tk // kc
    nchunk = seq // tk

    @pl.when(pl.program_id(2) == 0)
    def _():
        def norm_chunk(c, best):
            kk = k_ref[0, 0, pl.ds(pl.multiple_of(c * tk, tk), tk), :].astype(F32)
            return jnp.maximum(best, jnp.max(jnp.sum(kk * kk, axis=1, keepdims=True), axis=0, keepdims=True))
        kmax_ref[...] = jnp.sqrt(lax.fori_loop(0, nchunk, norm_chunk, jnp.zeros((1, 1), F32)))

    q_t = q_ref[0].astype(F32).T
    for g in range(GROUP):
        q_g = q_t[g * HEAD_DIM:(g + 1) * HEAD_DIM, :]
        qt_ref[:, g * tq:(g + 1) * tq] = q_g.astype(BF16)
        shift_ref[:, g * tq:(g + 1) * tq] = jnp.sqrt(jnp.sum(q_g * q_g, axis=0, keepdims=True)) * kmax_ref[...]
    acc_ref[...] = jnp.zeros_like(acc_ref)
    safe = jnp.max(shift_ref[...]) <= SAFE_SHIFT

    def pv(c, p_t):
        vt = jnp.concatenate([vt_ref[0, 0, c * nsub + sub] for sub in range(nsub)], axis=1)
        return jnp.dot(vt, p_t, preferred_element_type=F32)

    def scores(c):
        k = k_ref[0, 0, pl.ds(pl.multiple_of(c * tk, tk), tk), :]
        return jnp.dot(k, qt_ref[...], preferred_element_type=F32)

    @pl.when(safe)
    def _():
        def body(c, carry):
            p_t = jnp.exp2(scores(c) - shift_ref[...]).astype(BF16)
            acc_ref[...] += pv(c, p_t)
            return carry
        lax.fori_loop(0, nchunk, body, 0, unroll=min(ATT_UNROLL, nchunk))

    @pl.when(jnp.logical_not(safe))
    def _():
        m_ref[...] = jnp.full_like(m_ref, -jnp.inf)

        def body(c, carry):
            s_t = scores(c)
            m_prev = m_ref[...]
            m_new = jnp.maximum(m_prev, jnp.max(s_t, axis=0, keepdims=True))
            p_t = jnp.exp2(s_t - m_new).astype(BF16)
            acc_ref[...] = jnp.exp2(m_prev - m_new) * acc_ref[...] + pv(c, p_t)
            m_ref[...] = m_new
            return carry
        lax.fori_loop(0, nchunk, body, 0)

    out_t = acc_ref[0:HEAD_DIM, :] / acc_ref[HEAD_DIM:HEAD_DIM + 1, :]
    out = jnp.concatenate([out_t[:, g * tq:(g + 1) * tq] for g in range(GROUP)], axis=0).T
    o_ref[0] = out.astype(o_ref.dtype)


def _attention(q, k, vt):
    b, s, q_cols = q.shape
    tq = min(ATT_TQ, s)
    tk = min(ATT_TK, s)
    gw = GROUP * HEAD_DIM
    nchunk, vt_rows, kc = vt.shape[2:]
    return pl.pallas_call(
        functools.partial(_attn_kernel, tk=tk),
        grid=(b, N_KV_HEADS, s // tq),
        in_specs=[
            pl.BlockSpec((1, tq, gw), lambda bi, j, i: (bi, i, j)),
            pl.BlockSpec((1, 1, s, HEAD_DIM), lambda bi, j, i: (bi, j, 0, 0)),
            pl.BlockSpec((1, 1, nchunk, vt_rows, kc), lambda bi, j, i: (bi, j, 0, 0, 0)),
        ],
        out_specs=pl.BlockSpec((1, tq, gw), lambda bi, j, i: (bi, i, j)),
        out_shape=jax.ShapeDtypeStruct((b, s, q_cols), BF16),
        scratch_shapes=[
            pltpu.VMEM((HEAD_DIM, GROUP * tq), BF16),
            pltpu.VMEM((1, GROUP * tq), F32),
            pltpu.VMEM((1, GROUP * tq), F32),
            pltpu.VMEM((vt_rows, GROUP * tq), F32),
            pltpu.VMEM((1, 1), F32),
        ],
        compiler_params=_params("parallel", "arbitrary", "arbitrary"),
        name="attention",
    )(q, k, vt)


def _lru_gates_and_scan(xc, wcat_ref, ba_ref, bx_ref, lam_ref, a_ref, u_ref, hs_ref, hc_ref, reverse):
    tm, d = xc.shape
    rb = min(RNN_ROW_BLOCK, tm)
    lam = lam_ref[...]
    softplus_neg_lam = jnp.maximum(-lam, 0.0) + jnp.log1p(jnp.exp(-jnp.abs(lam)))
    decay_rate = (-LRU_C * LOG2E) * softplus_neg_lam
    h = hc_ref[...]
    order = range(tm // rb - 1, -1, -1) if reverse else range(tm // rb)
    for blk in order:
        rows = slice(blk * rb, (blk + 1) * rb)
        xb = xc[rows, :]
        xbb = xb.astype(BF16)
        for c in range(d // MXU_DIM):
            sl = slice(c * MXU_DIM, (c + 1) * MXU_DIM)
            g = jnp.dot(xbb[:, sl], wcat_ref[c], preferred_element_type=F32)
            r = _sigmoid(g[:, :MXU_DIM] + ba_ref[:, sl])
            i = _sigmoid(g[:, MXU_DIM:] + bx_ref[:, sl])
            a = jnp.exp2(r * decay_rate[:, sl])
            a_ref[rows, sl] = a
            y = 1.0 - a * a
            u_ref[rows, sl] = y * lax.rsqrt(jnp.maximum(y, 1e-30)) * (i * xb[:, sl])
        for row in (range((blk + 1) * rb - 1, blk * rb - 1, -1) if reverse else range(blk * rb, (blk + 1) * rb)):
            h = a_ref[row:row + 1, :] * h + u_ref[row:row + 1, :]
            hs_ref[row:row + 1, :] = h
    hc_ref[...] = h


def _rnn_fwd_kernel(xr_ref, prev_ref, next_ref, cw_ref, cb_ref, wcat_ref, ba_ref, bx_ref, lam_ref,
                    hf_ref, xc_ref, xpad_ref, a_ref, u_ref, hc_ref):
    t = pl.program_id(1)
    nt = pl.num_programs(1)
    tm = xr_ref.shape[1]
    halo = SUBLANES

    @pl.when(t == 0)
    def _():
        hc_ref[...] = jnp.zeros_like(hc_ref)

    xpad_ref[halo:halo + tm, :] = xr_ref[0]
    xpad_ref[0:halo, :] = jnp.where(t > 0, prev_ref[0], 0.0)
    xpad_ref[halo + tm:halo + tm + halo, :] = jnp.where(t < nt - 1, next_ref[0], 0.0)
    xc = cb_ref[...] + sum(
        cw_ref[kk:kk + 1, :] * xpad_ref[halo - CONV_LEFT + kk:halo - CONV_LEFT + kk + tm, :]
        for kk in range(CONV_W))
    xc_ref[0] = xc
    _lru_gates_and_scan(xc, wcat_ref, ba_ref, bx_ref, lam_ref, a_ref, u_ref, hf_ref.at[0], hc_ref, reverse=False)


def _rnn_bwd_kernel(xc_ref, wcat_ref, ba_ref, bx_ref, lam_ref, hf_ref, gy_ref, o_ref, a_ref, u_ref, hs_ref, hc_ref):
    @pl.when(pl.program_id(1) == 0)
    def _():
        hc_ref[...] = jnp.zeros_like(hc_ref)

    _lru_gates_and_scan(xc_ref[0], wcat_ref, ba_ref, bx_ref, lam_ref, a_ref, u_ref, hs_ref, hc_ref, reverse=True)
    o_ref[0] = ((hf_ref[0] + hs_ref[...]) * gy_ref[0].astype(F32)).astype(o_ref.dtype)


def _gate_slabs(wa, wx, d):
    nb, blk, _ = wa.shape
    per = MXU_DIM // blk
    nslab = d // MXU_DIM

    def slab_diag(w):
        w = w.reshape(nslab, per, blk, blk)
        eye = jnp.eye(per, dtype=w.dtype)
        return jnp.einsum('spij,pq->spiqj', w, eye).reshape(nslab, MXU_DIM, MXU_DIM)

    return jnp.concatenate([slab_diag(wa), slab_diag(wx)], axis=2).astype(BF16)


def _rnn_fwd(xr, conv_w, conv_b, wa, ba, wx, bx, lam):
    b, s, d = xr.shape
    tm = min(RNN_TM, s)
    nt = s // tm
    hb = tm // SUBLANES
    wcat = _gate_slabs(wa, wx, d)
    main = pl.BlockSpec((1, tm, d), lambda bi, t: (bi, t, 0))
    const = lambda shape: pl.BlockSpec(shape, lambda bi, t: (0,) * len(shape))
    return pl.pallas_call(
        _rnn_fwd_kernel,
        grid=(b, nt),
        in_specs=[
            main,
            pl.BlockSpec((1, SUBLANES, d), lambda bi, t: (bi, jnp.maximum(t * hb - 1, 0), 0)),
            pl.BlockSpec((1, SUBLANES, d), lambda bi, t: (bi, jnp.minimum((t + 1) * hb, s // SUBLANES - 1), 0)),
            const((CONV_W, d)), const((1, d)), const(wcat.shape), const((1, d)), const((1, d)), const((1, d)),
        ],
        out_specs=[main, main],
        out_shape=[jax.ShapeDtypeStruct((b, s, d), F32), jax.ShapeDtypeStruct((b, s, d), F32)],
        scratch_shapes=[
            pltpu.VMEM((tm + 2 * SUBLANES, d), F32),
            pltpu.VMEM((tm, d), F32),
            pltpu.VMEM((tm, d), F32),
            pltpu.VMEM((1, d), F32),
        ],
        compiler_params=_params("parallel", "arbitrary"),
        name="rnn_fwd",
    )(xr, xr, xr, conv_w, conv_b.reshape(1, d), wcat, ba.reshape(1, d), bx.reshape(1, d), lam.reshape(1, d))


def _rnn_bwd(xc, wa, ba, wx, bx, lam, hf, gy):
    b, s, d = xc.shape
    tm = min(RNN_TM, s)
    nt = s // tm
    wcat = _gate_slabs(wa, wx, d)
    main = pl.BlockSpec((1, tm, d), lambda bi, t: (bi, nt - 1 - t, 0))
    const = lambda shape: pl.BlockSpec(shape, lambda bi, t: (0,) * len(shape))
    return pl.pallas_call(
        _rnn_bwd_kernel,
        grid=(b, nt),
        in_specs=[main, const(wcat.shape), const((1, d)), const((1, d)), const((1, d)), main, main],
        out_specs=main,
        out_shape=jax.ShapeDtypeStruct((b, s, d), BF16),
        scratch_shapes=[
            pltpu.VMEM((tm, d), F32),
            pltpu.VMEM((tm, d), F32),
            pltpu.VMEM((tm, d), F32),
            pltpu.VMEM((1, d), F32),
        ],
        compiler_params=_params("parallel", "arbitrary"),
        name="rnn_bwd",
    )(xc, wcat, ba.reshape(1, d), bx.reshape(1, d), lam.reshape(1, d), hf, gy)


def _merge_kernel(x_ref, attn_ref, rg_ref, gate_ref, wa_ref, wr_ref, wo_ref, o_ref):
    d = x_ref.shape[1]
    a = jnp.dot(attn_ref[...], wa_ref[...], preferred_element_type=F32)
    r = jnp.dot(rg_ref[...], wr_ref[...], preferred_element_type=F32)
    merged = gate_ref[:, :d].astype(F32) * a + gate_ref[:, d:].astype(F32) * r
    o_ref[...] = x_ref[...] + jnp.dot(merged.astype(BF16), wo_ref[...], preferred_element_type=F32)


def _merge(x2, attn2, rg2, gate2, w_attn_o, w_rnn_o, w_out):
    t, d = x2.shape
    tm = min(OUT_TM, t)
    rows = lambda width: pl.BlockSpec((tm, width), lambda i: (i, 0))
    const = lambda shape: pl.BlockSpec(shape, lambda i: (0, 0))
    return pl.pallas_call(
        _merge_kernel,
        grid=(t // tm,),
        in_specs=[rows(d), rows(attn2.shape[1]), rows(rg2.shape[1]), rows(2 * d),
                  const(w_attn_o.shape), const(w_rnn_o.shape), const(w_out.shape)],
        out_specs=rows(d),
        out_shape=jax.ShapeDtypeStruct((t, d), F32),
        compiler_params=_params("parallel"),
        name="merge",
    )(x2, attn2, rg2, gate2, w_attn_o, w_rnn_o, w_out)


def kernel(x, ffn1_norm, ffn1_w1, ffn1_w2, mix_norm, w_in, b_gate, q_norm, k_norm, w_attn_o, conv_w, conv_b,
           lru_wa, lru_ba, lru_wx, lru_bx, lru_lambda, w_rnn_o, w_out, ffn2_norm, ffn2_w1, ffn2_w2):
    b, s, d = x.shape
    depth = w_in.shape[0]
    x2 = x.reshape(b * s, d)
    for l in range(depth):
        x2 = _ffn(x2, ffn1_norm[l], ffn1_w1[l].astype(BF16), ffn1_w2[l].astype(BF16))
        q, k, vt, xr, gy, gates = _in_proj(x2.reshape(b, s, d), mix_norm[l], w_in[l].astype(BF16), b_gate[l],
                                          q_norm[l], k_norm[l])
        attn = _attention(q, k, vt)
        hf, xc = _rnn_fwd(xr, conv_w[l], conv_b[l], lru_wa[l, 0], lru_ba[l, 0], lru_wx[l, 0], lru_bx[l, 0],
                          lru_lambda[l, 0])
        rg = _rnn_bwd(xc, lru_wa[l, 1], lru_ba[l, 1], lru_wx[l, 1], lru_bx[l, 1], lru_lambda[l, 1], hf, gy)
        x2 = _merge(x2, attn.reshape(b * s, -1), rg.reshape(b * s, -1), gates.reshape(b * s, -1),
                    w_attn_o[l].astype(BF16), w_rnn_o[l].astype(BF16), w_out[l].astype(BF16))
        x2 = _ffn(x2, ffn2_norm[l], ffn2_w1[l].astype(BF16), ffn2_w2[l].astype(BF16))
    return x2.reshape(b, s, d)
```

```python
import functools
import math

import jax
import jax.numpy as jnp
from jax import lax
from jax.experimental import pallas as pl
from jax.experimental.pallas import tpu as pltpu

F32 = jnp.float32
BF16 = jnp.bfloat16

N_HEADS = 16
N_KV_HEADS = 4
HEAD_DIM = 64
GROUP = N_HEADS // N_KV_HEADS
VT_ROWS = HEAD_DIM + 16
SAFE_SHIFT = 60.0
ROPE_THETA = 10000.0
ROPE_QUARTER = HEAD_DIM // 4
GRID_W = 64
N_RNN_BLOCKS = 16
CONV_W = 4
CONV_LEFT = 2
LRU_C = 8.0
FFN_RESID = 0.5
EPS = 1e-6
LOG2E = math.log2(math.e)

LANES = 128
SUBLANES = 8
MXU_DIM = 256
VMEM_LIMIT_BYTES = 56 * 1024 * 1024

FFN_TM = 512
FFN_TF = 256
PROJ_TM = 512
PROJ_ROW_SPLIT = 2
ATT_TQ = 512
ATT_TK = 512
ATT_UNROLL = 2
RNN_TM = 512
RNN_ROW_BLOCK = 256
OUT_TM = 512


def _params(*sem):
    return pltpu.CompilerParams(dimension_semantics=sem, vmem_limit_bytes=VMEM_LIMIT_BYTES)


def _rms_scale(x):
    return lax.rsqrt(jnp.sum(x * x, axis=-1, keepdims=True) * (1.0 / x.shape[-1]) + EPS)


def _sigmoid(x):
    return 1.0 / (1.0 + jnp.exp2(x * -LOG2E))


def _ffn_kernel(x_ref, g_ref, w1_ref, w2_ref, o_ref):
    d_ff = w2_ref.shape[0]
    x = x_ref[...]
    h = (x * _rms_scale(x) * g_ref[...]).astype(BF16)
    acc = None
    for c in range(0, d_ff, FFN_TF):
        g = jnp.dot(h, w1_ref[:, c:c + FFN_TF], preferred_element_type=F32)
        u = jnp.dot(h, w1_ref[:, d_ff + c:d_ff + c + FFN_TF], preferred_element_type=F32)
        act = (g * _sigmoid(g) * u).astype(BF16)
        part = jnp.dot(act, w2_ref[c:c + FFN_TF, :], preferred_element_type=F32)
        acc = part if acc is None else acc + part
    o_ref[...] = x + FFN_RESID * acc


def _ffn(x2, gain, w1, w2):
    t, d = x2.shape
    tm = min(FFN_TM, t)
    resident = lambda shape: pl.BlockSpec(shape, lambda i: (0, 0), pipeline_mode=pl.Buffered(1))
    return pl.pallas_call(
        _ffn_kernel,
        grid=(t // tm,),
        in_specs=[
            pl.BlockSpec((tm, d), lambda i: (i, 0)),
            pl.BlockSpec((1, d), lambda i: (0, 0)),
            resident(w1.shape),
            resident(w2.shape),
        ],
        out_specs=pl.BlockSpec((tm, d), lambda i: (i, 0)),
        out_shape=jax.ShapeDtypeStruct((t, d), F32),
        compiler_params=_params("parallel"),
        name="ffn",
    )(x2, gain.reshape(1, d), w1, w2)


def _rope_tables(seq):
    pos = jnp.arange(seq, dtype=jnp.int32)
    row = (pos // GRID_W).astype(F32)
    col = (pos % GRID_W).astype(F32)
    half = HEAD_DIM // 2
    inv = ROPE_THETA ** (-jnp.arange(0, half, 2, dtype=F32) / half)
    ang_r = row[:, None] * inv[None, :]
    ang_c = col[:, None] * inv[None, :]
    zero = jnp.zeros_like(ang_r)
    cos = jnp.concatenate([jnp.cos(ang_r), jnp.cos(ang_r), jnp.cos(ang_c), jnp.cos(ang_c)], axis=1)
    s_up = jnp.concatenate([-jnp.sin(ang_r), zero, -jnp.sin(ang_c), zero], axis=1)
    s_dn = jnp.concatenate([zero, jnp.sin(ang_r), zero, jnp.sin(ang_c)], axis=1)
    rep = LANES // HEAD_DIM
    return tuple(jnp.tile(a, (1, rep)) for a in (cos, s_up, s_dn))


def _head_norm_rope(z, seg, gain, cos, s_up, s_dn):
    width = z.shape[1]
    sq = (z * z).astype(BF16)
    ms = jnp.concatenate(
        [jnp.dot(sq[:, c:c + MXU_DIM], seg, preferred_element_type=F32) for c in range(0, width, MXU_DIM)],
        axis=1)
    zn = z * lax.rsqrt(ms + EPS) * gain
    rep = width // LANES
    widen = lambda a: jnp.concatenate([a] * rep, axis=1)
    up = pltpu.roll(zn, width - ROPE_QUARTER, axis=1)
    dn = pltpu.roll(zn, ROPE_QUARTER, axis=1)
    return zn * widen(cos) + up * widen(s_up) + dn * widen(s_dn)


def _proj_kernel(x_ref, gn_ref, wq_ref, wk_ref, wv_ref, wx_ref, wy_ref, wg_ref, bg_ref,
                 gq_ref, gk_ref, seg_ref, cos_ref, sup_ref, sdn_ref,
                 q_ref, k_ref, vt_ref, xr_ref, gy_ref, gate_ref):
    tm = x_ref.shape[1]
    rb = tm // PROJ_ROW_SPLIT
    seg = seg_ref[...]
    qscale = HEAD_DIM ** -0.5 * LOG2E
    for blk in range(PROJ_ROW_SPLIT):
        rows = slice(blk * rb, (blk + 1) * rb)
        x = x_ref[0, rows, :]
        h = (x * _rms_scale(x) * gn_ref[...]).astype(BF16)
        cos, s_up, s_dn = cos_ref[rows, :], sup_ref[rows, :], sdn_ref[rows, :]

        q = jnp.dot(h, wq_ref[...], preferred_element_type=F32)
        q_ref[0, rows, :] = (_head_norm_rope(q, seg, gq_ref[...], cos, s_up, s_dn) * qscale).astype(BF16)

        k = jnp.dot(h, wk_ref[...], preferred_element_type=F32)
        k = _head_norm_rope(k, seg, gk_ref[...], cos, s_up, s_dn).astype(BF16)
        vt = jnp.dot(h, wv_ref[...], preferred_element_type=F32).T.astype(BF16)
        for j in range(N_KV_HEADS):
            k_ref[0, j, rows, :] = k[:, j * HEAD_DIM:(j + 1) * HEAD_DIM]
            vt_ref[0, j, 0, 0:HEAD_DIM, rows] = vt[j * HEAD_DIM:(j + 1) * HEAD_DIM, :]
            vt_ref[0, j, 0, HEAD_DIM:VT_ROWS, rows] = jnp.ones((VT_ROWS - HEAD_DIM, rb), BF16)

        xr_ref[0, rows, :] = jnp.dot(h, wx_ref[...], preferred_element_type=F32)

        y = jnp.dot(h, wy_ref[...], preferred_element_type=F32)
        gelu = 0.5 * y * (1.0 + jnp.tanh(math.sqrt(2.0 / math.pi) * (y + 0.044715 * (y * y * y))))
        gy_ref[0, rows, :] = gelu.astype(BF16)

        gl = jnp.dot(h, wg_ref[...], preferred_element_type=F32) + bg_ref[...]
        gate_ref[0, rows, :] = _sigmoid(gl).astype(BF16)


def _in_proj(x, gain, w_in, b_gate, q_gain, k_gain):
    b, s, d = x.shape
    q_cols = N_HEADS * HEAD_DIM
    kv_cols = N_KV_HEADS * HEAD_DIM
    d_rnn = (w_in.shape[1] - q_cols - 2 * kv_cols - 2 * d) // 2
    o = [0, q_cols, q_cols + kv_cols, q_cols + 2 * kv_cols, q_cols + 2 * kv_cols + d_rnn,
         q_cols + 2 * kv_cols + 2 * d_rnn, w_in.shape[1]]
    ws = [w_in[:, o[i]:o[i + 1]] for i in range(6)]
    tm = min(PROJ_TM, s)
    nt = s // tm
    cos, s_up, s_dn = _rope_tables(s)
    lane = jnp.arange(MXU_DIM)
    seg = jnp.where(lane[:, None] // HEAD_DIM == lane[None, :] // HEAD_DIM, 1.0 / HEAD_DIM, 0.0).astype(BF16)
    const = lambda shape: pl.BlockSpec(shape, lambda bi, ti: (0,) * len(shape))
    tab = pl.BlockSpec((tm, LANES), lambda bi, ti: (ti, 0))
    row = lambda width: pl.BlockSpec((1, tm, width), lambda bi, ti: (bi, ti, 0))
    head = pl.BlockSpec((1, N_KV_HEADS, tm, HEAD_DIM), lambda bi, ti: (bi, 0, ti, 0))
    head_t = pl.BlockSpec((1, N_KV_HEADS, 1, VT_ROWS, tm), lambda bi, ti: (bi, 0, ti, 0, 0))
    return pl.pallas_call(
        _proj_kernel,
        grid=(b, nt),
        in_specs=[row(d), const((1, d))] + [const(w.shape) for w in ws]
        + [const((1, 2 * d)), const((1, q_cols)), const((1, kv_cols)), const((MXU_DIM, MXU_DIM)), tab, tab, tab],
        out_specs=[row(q_cols), head, head_t, row(d_rnn), row(d_rnn), row(2 * d)],
        out_shape=[
            jax.ShapeDtypeStruct((b, s, q_cols), BF16),
            jax.ShapeDtypeStruct((b, N_KV_HEADS, s, HEAD_DIM), BF16),
            jax.ShapeDtypeStruct((b, N_KV_HEADS, nt, VT_ROWS, tm), BF16),
            jax.ShapeDtypeStruct((b, s, d_rnn), F32),
            jax.ShapeDtypeStruct((b, s, d_rnn), BF16),
            jax.ShapeDtypeStruct((b, s, 2 * d), BF16),
        ],
        compiler_params=_params("parallel", "parallel"),
        name="in_proj",
    )(x, gain.reshape(1, d), *ws, b_gate.reshape(1, 2 * d),
      jnp.tile(q_gain, N_HEADS).reshape(1, q_cols), jnp.tile(k_gain, N_KV_HEADS).reshape(1, kv_cols),
      seg, cos, s_up, s_dn)


def _attn_kernel(q_ref, k_ref, vt_ref, o_ref, qt_ref, shift_ref, m_ref, acc_ref, kmax_ref, *, tk):
    tq = q_ref.shape[1]
    seq = k_ref.shape[2]
    kc = vt_ref.shape[4]
    nsub = tk // kc
    nchunk = seq // tk

    @pl.when(pl.program_id(2) == 0)
    def _():
        def norm_chunk(c, best):
            kk = k_ref[0, 0, pl.ds(pl.multiple_of(c * tk, tk), tk), :].astype(F32)
            return jnp.maximum(best, jnp.max(jnp.sum(kk * kk, axis=1, keepdims=True), axis=0, keepdims=True))
        kmax_ref[...] = jnp.sqrt(lax.fori_loop(0, nchunk, norm_chunk, jnp.zeros((1, 1), F32)))

    q_t = q_ref[0].astype(F32).T
    for g in range(GROUP):
        q_g = q_t[g * HEAD_DIM:(g + 1) * HEAD_DIM, :]
        qt_ref[:, g * tq:(g + 1) * tq] = q_g.astype(BF16)
        shift_ref[:, g * tq:(g + 1) * tq] = jnp.sqrt(jnp.sum(q_g * q_g, axis=0, keepdims=True)) * kmax_ref[...]
    acc_ref[...] = jnp.zeros_like(acc_ref)
    safe = jnp.max(shift_ref[...]) <= SAFE_SHIFT

    def pv(c, p_t):
        vt = jnp.concatenate([vt_ref[0, 0, c * nsub + sub] for sub in range(nsub)], axis=1)
        return jnp.dot(vt, p_t, preferred_element_type=F32)

    def scores(c):
        k = k_ref[0, 0, pl.ds(pl.multiple_of(c * tk, tk), tk), :]
        return jnp.dot(k, qt_ref[...], preferred_element_type=F32)

    @pl.when(safe)
    def _():
        def body(c, carry):
            p_t = jnp.exp2(scores(c) - shift_ref[...]).astype(BF16)
            acc_ref[...] += pv(c, p_t)
            return carry
        lax.fori_loop(0, nchunk, body, 0, unroll=min(ATT_UNROLL, nchunk))

    @pl.when(jnp.logical_not(safe))
    def _():
        m_ref[...] = jnp.full_like(m_ref, -jnp.inf)

        def body(c, carry):
            s_t = scores(c)
            m_prev = m_ref[...]
            m_new = jnp.maximum(m_prev, jnp.max(s_t, axis=0, keepdims=True))
            p_t = jnp.exp2(s_t - m_new).astype(BF16)
            acc_ref[...] = jnp.exp2(m_prev - m_new) * acc_ref[...] + pv(c, p_t)
            m_ref[...] = m_new
            return carry
        lax.fori_loop(0, nchunk, body, 0)

    out_t = acc_ref[0:HEAD_DIM, :] / acc_ref[HEAD_DIM:HEAD_DIM + 1, :]
    out = jnp.concatenate([out_t[:, g * tq:(g + 1) * tq] for g in range(GROUP)], axis=0).T
    o_ref[0] = out.astype(o_ref.dtype)


def _attention(q, k, vt):
    b, s, q_cols = q.shape
    tq = min(ATT_TQ, s)
    tk = min(ATT_TK, s)
    gw = GROUP * HEAD_DIM
    nchunk, vt_rows, kc = vt.shape[2:]
    return pl.pallas_call(
        functools.partial(_attn_kernel, tk=tk),
        grid=(b, N_KV_HEADS, s // tq),
        in_specs=[
            pl.BlockSpec((1, tq, gw), lambda bi, j, i: (bi, i, j)),
            pl.BlockSpec((1, 1, s, HEAD_DIM), lambda bi, j, i: (bi, j, 0, 0)),
            pl.BlockSpec((1, 1, nchunk, vt_rows, kc), lambda bi, j, i: (bi, j, 0, 0, 0)),
        ],
        out_specs=pl.BlockSpec((1, tq, gw), lambda bi, j, i: (bi, i, j)),
        out_shape=jax.ShapeDtypeStruct((b, s, q_cols), BF16),
        scratch_shapes=[
            pltpu.VMEM((HEAD_DIM, GROUP * tq), BF16),
            pltpu.VMEM((1, GROUP * tq), F32),
            pltpu.VMEM((1, GROUP * tq), F32),
            pltpu.VMEM((vt_rows, GROUP * tq), F32),
            pltpu.VMEM((1, 1), F32),
        ],
        compiler_params=_params("parallel", "arbitrary", "arbitrary"),
        name="attention",
    )(q, k, vt)


def _lru_gates_and_scan(xc, wcat_ref, ba_ref, bx_ref, lam_ref, a_ref, u_ref, hs_ref, hc_ref, reverse):
    tm, d = xc.shape
    rb = min(RNN_ROW_BLOCK, tm)
    lam = lam_ref[...]
    softplus_neg_lam = jnp.maximum(-lam, 0.0) + jnp.log1p(jnp.exp(-jnp.abs(lam)))
    decay_rate = (-LRU_C * LOG2E) * softplus_neg_lam
    h = hc_ref[...]
    order = range(tm // rb - 1, -1, -1) if reverse else range(tm // rb)
    for blk in order:
        rows = slice(blk * rb, (blk + 1) * rb)
        xb = xc[rows, :]
        xbb = xb.astype(BF16)
        for c in range(d // MXU_DIM):
            sl = slice(c * MXU_DIM, (c + 1) * MXU_DIM)
            g = jnp.dot(xbb[:, sl], wcat_ref[c], preferred_element_type=F32)
            r = _sigmoid(g[:, :MXU_DIM] + ba_ref[:, sl])
            i = _sigmoid(g[:, MXU_DIM:] + bx_ref[:, sl])
            a = jnp.exp2(r * decay_rate[:, sl])
            a_ref[rows, sl] = a
            y = 1.0 - a * a
            u_ref[rows, sl] = y * lax.rsqrt(jnp.maximum(y, 1e-30)) * (i * xb[:, sl])
        for row in (range((blk + 1) * rb - 1, blk * rb - 1, -1) if reverse else range(blk * rb, (blk + 1) * rb)):
            h = a_ref[row:row + 1, :] * h + u_ref[row:row + 1, :]
            hs_ref[row:row + 1, :] = h
    hc_ref[...] = h


def _rnn_fwd_kernel(xr_ref, prev_ref, next_ref, cw_ref, cb_ref, wcat_ref, ba_ref, bx_ref, lam_ref,
                    hf_ref, xc_ref, xpad_ref, a_ref, u_ref, hc_ref):
    t = pl.program_id(1)
    nt = pl.num_programs(1)
    tm = xr_ref.shape[1]
    halo = SUBLANES

    @pl.when(t == 0)
    def _():
        hc_ref[...] = jnp.zeros_like(hc_ref)

    xpad_ref[halo:halo + tm, :] = xr_ref[0]
    xpad_ref[0:halo, :] = jnp.where(t > 0, prev_ref[0], 0.0)
    xpad_ref[halo + tm:halo + tm + halo, :] = jnp.where(t < nt - 1, next_ref[0], 0.0)
    xc = cb_ref[...] + sum(
        cw_ref[kk:kk + 1, :] * xpad_ref[halo - CONV_LEFT + kk:halo - CONV_LEFT + kk + tm, :]
        for kk in range(CONV_W))
    xc_ref[0] = xc
    _lru_gates_and_scan(xc, wcat_ref, ba_ref, bx_ref, lam_ref, a_ref, u_ref, hf_ref.at[0], hc_ref, reverse=False)


def _rnn_bwd_kernel(xc_ref, wcat_ref, ba_ref, bx_ref, lam_ref, hf_ref, gy_ref, o_ref, a_ref, u_ref, hs_ref, hc_ref):
    @pl.when(pl.program_id(1) == 0)
    def _():
        hc_ref[...] = jnp.zeros_like(hc_ref)

    _lru_gates_and_scan(xc_ref[0], wcat_ref, ba_ref, bx_ref, lam_ref, a_ref, u_ref, hs_ref, hc_ref, reverse=True)
    o_ref[0] = ((hf_ref[0] + hs_ref[...]) * gy_ref[0].astype(F32)).astype(o_ref.dtype)


def _gate_slabs(wa, wx, d):
    nb, blk, _ = wa.shape
    per = MXU_DIM // blk
    nslab = d // MXU_DIM

    def slab_diag(w):
        w = w.reshape(nslab, per, blk, blk)
        eye = jnp.eye(per, dtype=w.dtype)
        return jnp.einsum('spij,pq->spiqj', w, eye).reshape(nslab, MXU_DIM, MXU_DIM)

    return jnp.concatenate([slab_diag(wa), slab_diag(wx)], axis=2).astype(BF16)


def _rnn_fwd(xr, conv_w, conv_b, wa, ba, wx, bx, lam):
    b, s, d = xr.shape
    tm = min(RNN_TM, s)
    nt = s // tm
    hb = tm // SUBLANES
    wcat = _gate_slabs(wa, wx, d)
    main = pl.BlockSpec((1, tm, d), lambda bi, t: (bi, t, 0))
    const = lambda shape: pl.BlockSpec(shape, lambda bi, t: (0,) * len(shape))
    return pl.pallas_call(
        _rnn_fwd_kernel,
        grid=(b, nt),
        in_specs=[
            main,
            pl.BlockSpec((1, SUBLANES, d), lambda bi, t: (bi, jnp.maximum(t * hb - 1, 0), 0)),
            pl.BlockSpec((1, SUBLANES, d), lambda bi, t: (bi, jnp.minimum((t + 1) * hb, s // SUBLANES - 1), 0)),
            const((CONV_W, d)), const((1, d)), const(wcat.shape), const((1, d)), const((1, d)), const((1, d)),
        ],
        out_specs=[main, main],
        out_shape=[jax.ShapeDtypeStruct((b, s, d), F32), jax.ShapeDtypeStruct((b, s, d), F32)],
        scratch_shapes=[
            pltpu.VMEM((tm + 2 * SUBLANES, d), F32),
            pltpu.VMEM((tm, d), F32),
            pltpu.VMEM((tm, d), F32),
            pltpu.VMEM((1, d), F32),
        ],
        compiler_params=_params("parallel", "arbitrary"),
        name="rnn_fwd",
    )(xr, xr, xr, conv_w, conv_b.reshape(1, d), wcat, ba.reshape(1, d), bx.reshape(1, d), lam.reshape(1, d))


def _rnn_bwd(xc, wa, ba, wx, bx, lam, hf, gy):
    b, s, d = xc.shape
    tm = min(RNN_TM, s)
    nt = s // tm
    wcat = _gate_slabs(wa, wx, d)
    main = pl.BlockSpec((1, tm, d), lambda bi, t: (bi, nt - 1 - t, 0))
    const = lambda shape: pl.BlockSpec(shape, lambda bi, t: (0,) * len(shape))
    return pl.pallas_call(
        _rnn_bwd_kernel,
        grid=(b, nt),
        in_specs=[main, const(wcat.shape), const((1, d)), const((1, d)), const((1, d)), main, main],
        out_specs=main,
        out_shape=jax.ShapeDtypeStruct((b, s, d), BF16),
        scratch_shapes=[
            pltpu.VMEM((tm, d), F32),
            pltpu.VMEM((tm, d), F32),
            pltpu.VMEM((tm, d), F32),
            pltpu.VMEM((1, d), F32),
        ],
        compiler_params=_params("parallel", "arbitrary"),
        name="rnn_bwd",
    )(xc, wcat, ba.reshape(1, d), bx.reshape(1, d), lam.reshape(1, d), hf, gy)


def _merge_kernel(x_ref, attn_ref, rg_ref, gate_ref, wa_ref, wr_ref, wo_ref, o_ref):
    d = x_ref.shape[1]
    a = jnp.dot(attn_ref[...], wa_ref[...], preferred_element_type=F32)
    r = jnp.dot(rg_ref[...], wr_ref[...], preferred_element_type=F32)
    merged = gate_ref[:, :d].astype(F32) * a + gate_ref[:, d:].astype(F32) * r
    o_ref[...] = x_ref[...] + jnp.dot(merged.astype(BF16), wo_ref[...], preferred_element_type=F32)


def _merge(x2, attn2, rg2, gate2, w_attn_o, w_rnn_o, w_out):
    t, d = x2.shape
    tm = min(OUT_TM, t)
    rows = lambda width: pl.BlockSpec((tm, width), lambda i: (i, 0))
    const = lambda shape: pl.BlockSpec(shape, lambda i: (0, 0))
    return pl.pallas_call(
        _merge_kernel,
        grid=(t // tm,),
        in_specs=[rows(d), rows(attn2.shape[1]), rows(rg2.shape[1]), rows(2 * d),
                  const(w_attn_o.shape), const(w_rnn_o.shape), const(w_out.shape)],
        out_specs=rows(d),
        out_shape=jax.ShapeDtypeStruct((t, d), F32),
        compiler_params=_params("parallel"),
        name="merge",
    )(x2, attn2, rg2, gate2, w_attn_o, w_rnn_o, w_out)


def kernel(x, ffn1_norm, ffn1_w1, ffn1_w2, mix_norm, w_in, b_gate, q_norm, k_norm, w_attn_o, conv_w, conv_b,
           lru_wa, lru_ba, lru_wx, lru_bx, lru_lambda, w_rnn_o, w_out, ffn2_norm, ffn2_w1, ffn2_w2):
    b, s, d = x.shape
    depth = w_in.shape[0]
    x2 = x.reshape(b * s, d)
    for l in range(depth):
        x2 = _ffn(x2, ffn1_norm[l], ffn1_w1[l].astype(BF16), ffn1_w2[l].astype(BF16))
        q, k, vt, xr, gy, gates = _in_proj(x2.reshape(b, s, d), mix_norm[l], w_in[l].astype(BF16), b_gate[l],
                                          q_norm[l], k_norm[l])
        attn = _attention(q, k, vt)
        hf, xc = _rnn_fwd(xr, conv_w[l], conv_b[l], lru_wa[l, 0], lru_ba[l, 0], lru_wx[l, 0], lru_bx[l, 0],
                          lru_lambda[l, 0])
        rg = _rnn_bwd(xc, lru_wa[l, 1], lru_ba[l, 1], lru_wx[l, 1], lru_bx[l, 1], lru_lambda[l, 1], hf, gy)
        x2 = _merge(x2, attn.reshape(b * s, -1), rg.reshape(b * s, -1), gates.reshape(b * s, -1),
                    w_attn_o[l].astype(BF16), w_rnn_o[l].astype(BF16), w_out[l].astype(BF16))
        x2 = _ffn(x2, ffn2_norm[l], ffn2_w1[l].astype(BF16), ffn2_w2[l].astype(BF16))
    return x2.reshape(b, s, d)
```

```python
import functools
import math

import jax
import jax.numpy as jnp
from jax import lax
from jax.experimental import pallas as pl
from jax.experimental.pallas import tpu as pltpu

F32 = jnp.float32
BF16 = jnp.bfloat16

N_HEADS = 16
N_KV_HEADS = 4
HEAD_DIM = 64
GROUP = N_HEADS // N_KV_HEADS
VT_ROWS = HEAD_DIM + 16
SAFE_SHIFT = 60.0
ROPE_THETA = 10000.0
ROPE_QUARTER = HEAD_DIM // 4
GRID_W = 64
N_RNN_BLOCKS = 16
CONV_W = 4
CONV_LEFT = 2
LRU_C = 8.0
FFN_RESID = 0.5
EPS = 1e-6
LOG2E = math.log2(math.e)

LANES = 128
SUBLANES = 8
MXU_DIM = 256
VMEM_LIMIT_BYTES = 56 * 1024 * 1024

FFN_TM = 512
FFN_TF = 256
PROJ_TM = 512
PROJ_ROW_SPLIT = 2
ATT_TQ = 512
ATT_TK = 512
ATT_UNROLL = 2
RNN_TM = 512
RNN_ROW_BLOCK = 256
OUT_TM = 1024


def _params(*sem):
    return pltpu.CompilerParams(dimension_semantics=sem, vmem_limit_bytes=VMEM_LIMIT_BYTES)


def _rms_scale(x):
    return lax.rsqrt(jnp.sum(x * x, axis=-1, keepdims=True) * (1.0 / x.shape[-1]) + EPS)


def _sigmoid(x):
    return 1.0 / (1.0 + jnp.exp2(x * -LOG2E))


def _ffn_kernel(x_ref, g_ref, w1_ref, w2_ref, o_ref):
    d_ff = w2_ref.shape[0]
    x = x_ref[...]
    h = (x * _rms_scale(x) * g_ref[...]).astype(BF16)
    acc = None
    for c in range(0, d_ff, FFN_TF):
        g = jnp.dot(h, w1_ref[:, c:c + FFN_TF], preferred_element_type=F32)
        u = jnp.dot(h, w1_ref[:, d_ff + c:d_ff + c + FFN_TF], preferred_element_type=F32)
        act = (g * _sigmoid(g) * u).astype(BF16)
        part = jnp.dot(act, w2_ref[c:c + FFN_TF, :], preferred_element_type=F32)
        acc = part if acc is None else acc + part
    o_ref[...] = x + FFN_RESID * acc


def _ffn(x2, gain, w1, w2):
    t, d = x2.shape
    tm = min(FFN_TM, t)
    resident = lambda shape: pl.BlockSpec(shape, lambda i: (0, 0), pipeline_mode=pl.Buffered(1))
    return pl.pallas_call(
        _ffn_kernel,
        grid=(t // tm,),
        in_specs=[
            pl.BlockSpec((tm, d), lambda i: (i, 0)),
            pl.BlockSpec((1, d), lambda i: (0, 0)),
            resident(w1.shape),
            resident(w2.shape),
        ],
        out_specs=pl.BlockSpec((tm, d), lambda i: (i, 0)),
        out_shape=jax.ShapeDtypeStruct((t, d), F32),
        compiler_params=_params("parallel"),
        name="ffn",
    )(x2, gain.reshape(1, d), w1, w2)


def _rope_tables(seq):
    pos = jnp.arange(seq, dtype=jnp.int32)
    row = (pos // GRID_W).astype(F32)
    col = (pos % GRID_W).astype(F32)
    half = HEAD_DIM // 2
    inv = ROPE_THETA ** (-jnp.arange(0, half, 2, dtype=F32) / half)
    ang_r = row[:, None] * inv[None, :]
    ang_c = col[:, None] * inv[None, :]
    zero = jnp.zeros_like(ang_r)
    cos = jnp.concatenate([jnp.cos(ang_r), jnp.cos(ang_r), jnp.cos(ang_c), jnp.cos(ang_c)], axis=1)
    s_up = jnp.concatenate([-jnp.sin(ang_r), zero, -jnp.sin(ang_c), zero], axis=1)
    s_dn = jnp.concatenate([zero, jnp.sin(ang_r), zero, jnp.sin(ang_c)], axis=1)
    rep = LANES // HEAD_DIM
    return tuple(jnp.tile(a, (1, rep)) for a in (cos, s_up, s_dn))


def _head_norm_rope(z, seg, gain, cos, s_up, s_dn):
    width = z.shape[1]
    sq = (z * z).astype(BF16)
    ms = jnp.concatenate(
        [jnp.dot(sq[:, c:c + MXU_DIM], seg, preferred_element_type=F32) for c in range(0, width, MXU_DIM)],
        axis=1)
    zn = z * lax.rsqrt(ms + EPS) * gain
    rep = width // LANES
    widen = lambda a: jnp.concatenate([a] * rep, axis=1)
    up = pltpu.roll(zn, width - ROPE_QUARTER, axis=1)
    dn = pltpu.roll(zn, ROPE_QUARTER, axis=1)
    return zn * widen(cos) + up * widen(s_up) + dn * widen(s_dn)


def _proj_kernel(x_ref, gn_ref, w_ref, bg_ref, gq_ref, gk_ref, seg_ref, cos_ref, sup_ref, sdn_ref,
                 q_ref, k_ref, vt_ref, xr_ref, gy_ref, gate_ref):
    tm = x_ref.shape[1]
    rb = tm // PROJ_ROW_SPLIT
    widths = (q_ref.shape[2], k_ref.shape[1] * HEAD_DIM, k_ref.shape[1] * HEAD_DIM,
              xr_ref.shape[2], gy_ref.shape[2], gate_ref.shape[2])
    starts = [sum(widths[:i]) for i in range(len(widths))]
    wq_ref, wk_ref, wv_ref, wx_ref, wy_ref, wg_ref = (w_ref.at[:, o:o + n] for o, n in zip(starts, widths))
    seg = seg_ref[...]
    qscale = HEAD_DIM ** -0.5 * LOG2E
    for blk in range(PROJ_ROW_SPLIT):
        rows = slice(blk * rb, (blk + 1) * rb)
        x = x_ref[0, rows, :]
        h = (x * _rms_scale(x) * gn_ref[...]).astype(BF16)
        cos, s_up, s_dn = cos_ref[rows, :], sup_ref[rows, :], sdn_ref[rows, :]

        q = jnp.dot(h, wq_ref[...], preferred_element_type=F32)
        q_ref[0, rows, :] = (_head_norm_rope(q, seg, gq_ref[...], cos, s_up, s_dn) * qscale).astype(BF16)

        k = jnp.dot(h, wk_ref[...], preferred_element_type=F32)
        k = _head_norm_rope(k, seg, gk_ref[...], cos, s_up, s_dn).astype(BF16)
        vt = jnp.dot(h, wv_ref[...], preferred_element_type=F32).T.astype(BF16)
        for j in range(N_KV_HEADS):
            k_ref[0, j, rows, :] = k[:, j * HEAD_DIM:(j + 1) * HEAD_DIM]
            vt_ref[0, j, 0, 0:HEAD_DIM, rows] = vt[j * HEAD_DIM:(j + 1) * HEAD_DIM, :]
            vt_ref[0, j, 0, HEAD_DIM:VT_ROWS, rows] = jnp.ones((VT_ROWS - HEAD_DIM, rb), BF16)

        xr_ref[0, rows, :] = jnp.dot(h, wx_ref[...], preferred_element_type=F32)

        y = jnp.dot(h, wy_ref[...], preferred_element_type=F32)
        gelu = 0.5 * y * (1.0 + jnp.tanh(math.sqrt(2.0 / math.pi) * (y + 0.044715 * (y * y * y))))
        gy_ref[0, rows, :] = gelu.astype(BF16)

        gl = jnp.dot(h, wg_ref[...], preferred_element_type=F32) + bg_ref[...]
        gate_ref[0, rows, :] = _sigmoid(gl).astype(BF16)


def _in_proj(x, gain, w_in, b_gate, q_gain, k_gain):
    b, s, d = x.shape
    q_cols = N_HEADS * HEAD_DIM
    kv_cols = N_KV_HEADS * HEAD_DIM
    d_rnn = (w_in.shape[1] - q_cols - 2 * kv_cols - 2 * d) // 2
    tm = min(PROJ_TM, s)
    nt = s // tm
    cos, s_up, s_dn = _rope_tables(s)
    lane = jnp.arange(MXU_DIM)
    seg = jnp.where(lane[:, None] // HEAD_DIM == lane[None, :] // HEAD_DIM, 1.0 / HEAD_DIM, 0.0).astype(BF16)
    const = lambda shape: pl.BlockSpec(shape, lambda bi, ti: (0,) * len(shape))
    tab = pl.BlockSpec((tm, LANES), lambda bi, ti: (ti, 0))
    row = lambda width: pl.BlockSpec((1, tm, width), lambda bi, ti: (bi, ti, 0))
    head = pl.BlockSpec((1, N_KV_HEADS, tm, HEAD_DIM), lambda bi, ti: (bi, 0, ti, 0))
    head_t = pl.BlockSpec((1, N_KV_HEADS, 1, VT_ROWS, tm), lambda bi, ti: (bi, 0, ti, 0, 0))
    return pl.pallas_call(
        _proj_kernel,
        grid=(b, nt),
        in_specs=[row(d), const((1, d)),
                  pl.BlockSpec(w_in.shape, lambda bi, ti: (0, 0), pipeline_mode=pl.Buffered(1)),
                  const((1, 2 * d)), const((1, q_cols)), const((1, kv_cols)), const((MXU_DIM, MXU_DIM)), tab, tab, tab],
        out_specs=[row(q_cols), head, head_t, row(d_rnn), row(d_rnn), row(2 * d)],
        out_shape=[
            jax.ShapeDtypeStruct((b, s, q_cols), BF16),
            jax.ShapeDtypeStruct((b, N_KV_HEADS, s, HEAD_DIM), BF16),
            jax.ShapeDtypeStruct((b, N_KV_HEADS, nt, VT_ROWS, tm), BF16),
            jax.ShapeDtypeStruct((b, s, d_rnn), F32),
            jax.ShapeDtypeStruct((b, s, d_rnn), BF16),
            jax.ShapeDtypeStruct((b, s, 2 * d), BF16),
        ],
        compiler_params=_params("parallel", "parallel"),
        name="in_proj",
    )(x, gain.reshape(1, d), w_in, b_gate.reshape(1, 2 * d),
      jnp.tile(q_gain, N_HEADS).reshape(1, q_cols), jnp.tile(k_gain, N_KV_HEADS).reshape(1, kv_cols),
      seg, cos, s_up, s_dn)


def _attn_kernel(q_ref, k_ref, vt_ref, o_ref, qt_ref, shift_ref, m_ref, acc_ref, kmax_ref, *, tk):
    tq = q_ref.shape[1]
    seq = k_ref.shape[2]
    kc = vt_ref.shape[4]
    nsub = tk // kc
    nchunk = seq // tk

    @pl.when(pl.program_id(2) == 0)
    def _():
        def norm_chunk(c, best):
            kk = k_ref[0, 0, pl.ds(pl.multiple_of(c * tk, tk), tk), :].astype(F32)
            return jnp.maximum(best, jnp.max(jnp.sum(kk * kk, axis=1, keepdims=True), axis=0, keepdims=True))
        kmax_ref[...] = jnp.sqrt(lax.fori_loop(0, nchunk, norm_chunk, jnp.zeros((1, 1), F32)))

    q_t = q_ref[0].astype(F32).T
    for g in range(GROUP):
        q_g = q_t[g * HEAD_DIM:(g + 1) * HEAD_DIM, :]
        qt_ref[:, g * tq:(g + 1) * tq] = q_g.astype(BF16)
        shift_ref[:, g * tq:(g + 1) * tq] = jnp.sqrt(jnp.sum(q_g * q_g, axis=0, keepdims=True)) * kmax_ref[...]
    acc_ref[...] = jnp.zeros_like(acc_ref)
    safe = jnp.max(shift_ref[...]) <= SAFE_SHIFT

    def pv(c, p_t):
        vt = jnp.concatenate([vt_ref[0, 0, c * nsub + sub] for sub in range(nsub)], axis=1)
        return jnp.dot(vt, p_t, preferred_element_type=F32)

    def scores(c):
        k = k_ref[0, 0, pl.ds(pl.multiple_of(c * tk, tk), tk), :]
        return jnp.dot(k, qt_ref[...], preferred_element_type=F32)

    @pl.when(safe)
    def _():
        def body(c, carry):
            p_t = jnp.exp2(scores(c) - shift_ref[...]).astype(BF16)
            acc_ref[...] += pv(c, p_t)
            return carry
        lax.fori_loop(0, nchunk, body, 0, unroll=min(ATT_UNROLL, nchunk))

    @pl.when(jnp.logical_not(safe))
    def _():
        m_ref[...] = jnp.full_like(m_ref, -jnp.inf)

        def body(c, carry):
            s_t = scores(c)
            m_prev = m_ref[...]
            m_new = jnp.maximum(m_prev, jnp.max(s_t, axis=0, keepdims=True))
            p_t = jnp.exp2(s_t - m_new).astype(BF16)
            acc_ref[...] = jnp.exp2(m_prev - m_new) * acc_ref[...] + pv(c, p_t)
            m_ref[...] = m_new
            return carry
        lax.fori_loop(0, nchunk, body, 0)

    out_t = acc_ref[0:HEAD_DIM, :] / acc_ref[HEAD_DIM:HEAD_DIM + 1, :]
    out = jnp.concatenate([out_t[:, g * tq:(g + 1) * tq] for g in range(GROUP)], axis=0).T
    o_ref[0] = out.astype(o_ref.dtype)


def _attention(q, k, vt):
    b, s, q_cols = q.shape
    tq = min(ATT_TQ, s)
    tk = min(ATT_TK, s)
    gw = GROUP * HEAD_DIM
    nchunk, vt_rows, kc = vt.shape[2:]
    return pl.pallas_call(
        functools.partial(_attn_kernel, tk=tk),
        grid=(b, N_KV_HEADS, s // tq),
        in_specs=[
            pl.BlockSpec((1, tq, gw), lambda bi, j, i: (bi, i, j)),
            pl.BlockSpec((1, 1, s, HEAD_DIM), lambda bi, j, i: (bi, j, 0, 0)),
            pl.BlockSpec((1, 1, nchunk, vt_rows, kc), lambda bi, j, i: (bi, j, 0, 0, 0)),
        ],
        out_specs=pl.BlockSpec((1, tq, gw), lambda bi, j, i: (bi, i, j)),
        out_shape=jax.ShapeDtypeStruct((b, s, q_cols), BF16),
        scratch_shapes=[
            pltpu.VMEM((HEAD_DIM, GROUP * tq), BF16),
            pltpu.VMEM((1, GROUP * tq), F32),
            pltpu.VMEM((1, GROUP * tq), F32),
            pltpu.VMEM((vt_rows, GROUP * tq), F32),
            pltpu.VMEM((1, 1), F32),
        ],
        compiler_params=_params("parallel", "arbitrary", "arbitrary"),
        name="attention",
    )(q, k, vt)


def _lru_gates_and_scan(xc, wcat_ref, ba_ref, bx_ref, lam_ref, a_ref, u_ref, hs_ref, hc_ref, reverse):
    tm, d = xc.shape
    rb = min(RNN_ROW_BLOCK, tm)
    lam = lam_ref[...]
    softplus_neg_lam = jnp.maximum(-lam, 0.0) + jnp.log1p(jnp.exp(-jnp.abs(lam)))
    decay_rate = (-LRU_C * LOG2E) * softplus_neg_lam
    h = hc_ref[...]
    order = range(tm // rb - 1, -1, -1) if reverse else range(tm // rb)
    for blk in order:
        rows = slice(blk * rb, (blk + 1) * rb)
        xb = xc[rows, :]
        xbb = xb.astype(BF16)
        for c in range(d // MXU_DIM):
            sl = slice(c * MXU_DIM, (c + 1) * MXU_DIM)
            g = jnp.dot(xbb[:, sl], wcat_ref[c], preferred_element_type=F32)
            r = _sigmoid(g[:, :MXU_DIM] + ba_ref[:, sl])
            i = _sigmoid(g[:, MXU_DIM:] + bx_ref[:, sl])
            a = jnp.exp2(r * decay_rate[:, sl])
            a_ref[rows, sl] = a
            y = 1.0 - a * a
            u_ref[rows, sl] = y * lax.rsqrt(jnp.maximum(y, 1e-30)) * (i * xb[:, sl])
        for row in (range((blk + 1) * rb - 1, blk * rb - 1, -1) if reverse else range(blk * rb, (blk + 1) * rb)):
            h = a_ref[row:row + 1, :] * h + u_ref[row:row + 1, :]
            hs_ref[row:row + 1, :] = h
    hc_ref[...] = h


def _rnn_fwd_kernel(xr_ref, prev_ref, next_ref, cw_ref, cb_ref, wcat_ref, ba_ref, bx_ref, lam_ref,
                    hf_ref, xc_ref, xpad_ref, a_ref, u_ref, hc_ref):
    t = pl.program_id(1)
    nt = pl.num_programs(1)
    tm = xr_ref.shape[1]
    halo = SUBLANES

    @pl.when(t == 0)
    def _():
        hc_ref[...] = jnp.zeros_like(hc_ref)

    xpad_ref[halo:halo + tm, :] = xr_ref[0]
    xpad_ref[0:halo, :] = jnp.where(t > 0, prev_ref[0], 0.0)
    xpad_ref[halo + tm:halo + tm + halo, :] = jnp.where(t < nt - 1, next_ref[0], 0.0)
    xc = cb_ref[...] + sum(
        cw_ref[kk:kk + 1, :] * xpad_ref[halo - CONV_LEFT + kk:halo - CONV_LEFT + kk + tm, :]
        for kk in range(CONV_W))
    xc_ref[0] = xc
    _lru_gates_and_scan(xc, wcat_ref, ba_ref, bx_ref, lam_ref, a_ref, u_ref, hf_ref.at[0], hc_ref, reverse=False)


def _rnn_bwd_kernel(xc_ref, wcat_ref, ba_ref, bx_ref, lam_ref, hf_ref, gy_ref, o_ref, a_ref, u_ref, hs_ref, hc_ref):
    @pl.when(pl.program_id(1) == 0)
    def _():
        hc_ref[...] = jnp.zeros_like(hc_ref)

    _lru_gates_and_scan(xc_ref[0], wcat_ref, ba_ref, bx_ref, lam_ref, a_ref, u_ref, hs_ref, hc_ref, reverse=True)
    o_ref[0] = ((hf_ref[0] + hs_ref[...]) * gy_ref[0].astype(F32)).astype(o_ref.dtype)


def _gate_slabs(wa, wx, d):
    nb, blk, _ = wa.shape
    per = MXU_DIM // blk
    nslab = d // MXU_DIM

    def slab_diag(w):
        w = w.reshape(nslab, per, blk, blk)
        eye = jnp.eye(per, dtype=w.dtype)
        return jnp.einsum('spij,pq->spiqj', w, eye).reshape(nslab, MXU_DIM, MXU_DIM)

    return jnp.concatenate([slab_diag(wa), slab_diag(wx)], axis=2).astype(BF16)


def _rnn_fwd(xr, conv_w, conv_b, wa, ba, wx, bx, lam):
    b, s, d = xr.shape
    tm = min(RNN_TM, s)
    nt = s // tm
    hb = tm // SUBLANES
    wcat = _gate_slabs(wa, wx, d)
    main = pl.BlockSpec((1, tm, d), lambda bi, t: (bi, t, 0))
    const = lambda shape: pl.BlockSpec(shape, lambda bi, t: (0,) * len(shape))
    return pl.pallas_call(
        _rnn_fwd_kernel,
        grid=(b, nt),
        in_specs=[
            main,
            pl.BlockSpec((1, SUBLANES, d), lambda bi, t: (bi, jnp.maximum(t * hb - 1, 0), 0)),
            pl.BlockSpec((1, SUBLANES, d), lambda bi, t: (bi, jnp.minimum((t + 1) * hb, s // SUBLANES - 1), 0)),
            const((CONV_W, d)), const((1, d)), const(wcat.shape), const((1, d)), const((1, d)), const((1, d)),
        ],
        out_specs=[main, main],
        out_shape=[jax.ShapeDtypeStruct((b, s, d), F32), jax.ShapeDtypeStruct((b, s, d), F32)],
        scratch_shapes=[
            pltpu.VMEM((tm + 2 * SUBLANES, d), F32),
            pltpu.VMEM((tm, d), F32),
            pltpu.VMEM((tm, d), F32),
            pltpu.VMEM((1, d), F32),
        ],
        compiler_params=_params("parallel", "arbitrary"),
        name="rnn_fwd",
    )(xr, xr, xr, conv_w, conv_b.reshape(1, d), wcat, ba.reshape(1, d), bx.reshape(1, d), lam.reshape(1, d))


def _rnn_bwd(xc, wa, ba, wx, bx, lam, hf, gy):
    b, s, d = xc.shape
    tm = min(RNN_TM, s)
    nt = s // tm
    wcat = _gate_slabs(wa, wx, d)
    main = pl.BlockSpec((1, tm, d), lambda bi, t: (bi, nt - 1 - t, 0))
    const = lambda shape: pl.BlockSpec(shape, lambda bi, t: (0,) * len(shape))
    return pl.pallas_call(
        _rnn_bwd_kernel,
        grid=(b, nt),
        in_specs=[main, const(wcat.shape), const((1, d)), const((1, d)), const((1, d)), main, main],
        out_specs=main,
        out_shape=jax.ShapeDtypeStruct((b, s, d), BF16),
        scratch_shapes=[
            pltpu.VMEM((tm, d), F32),
            pltpu.VMEM((tm, d), F32),
            pltpu.VMEM((tm, d), F32),
            pltpu.VMEM((1, d), F32),
        ],
        compiler_params=_params("parallel", "arbitrary"),
        name="rnn_bwd",
    )(xc, wcat, ba.reshape(1, d), bx.reshape(1, d), lam.reshape(1, d), hf, gy)


def _merge_kernel(x_ref, attn_ref, rg_ref, gate_ref, wa_ref, wr_ref, wo_ref, o_ref):
    d = x_ref.shape[1]
    a = jnp.dot(attn_ref[...], wa_ref[...], preferred_element_type=F32)
    r = jnp.dot(rg_ref[...], wr_ref[...], preferred_element_type=F32)
    merged = gate_ref[:, :d].astype(F32) * a + gate_ref[:, d:].astype(F32) * r
    o_ref[...] = x_ref[...] + jnp.dot(merged.astype(BF16), wo_ref[...], preferred_element_type=F32)


def _merge(x2, attn2, rg2, gate2, w_attn_o, w_rnn_o, w_out):
    t, d = x2.shape
    tm = min(OUT_TM, t)
    rows = lambda width: pl.BlockSpec((tm, width), lambda i: (i, 0))
    const = lambda shape: pl.BlockSpec(shape, lambda i: (0, 0))
    return pl.pallas_call(
        _merge_kernel,
        grid=(t // tm,),
        in_specs=[rows(d), rows(attn2.shape[1]), rows(rg2.shape[1]), rows(2 * d),
                  const(w_attn_o.shape), const(w_rnn_o.shape), const(w_out.shape)],
        out_specs=rows(d),
        out_shape=jax.ShapeDtypeStruct((t, d), F32),
        compiler_params=_params("parallel"),
        name="merge",
    )(x2, attn2, rg2, gate2, w_attn_o, w_rnn_o, w_out)


def kernel(x, ffn1_norm, ffn1_w1, ffn1_w2, mix_norm, w_in, b_gate, q_norm, k_norm, w_attn_o, conv_w, conv_b,
           lru_wa, lru_ba, lru_wx, lru_bx, lru_lambda, w_rnn_o, w_out, ffn2_norm, ffn2_w1, ffn2_w2):
    b, s, d = x.shape
    depth = w_in.shape[0]
    x2 = x.reshape(b * s, d)
    for l in range(depth):
        x2 = _ffn(x2, ffn1_norm[l], ffn1_w1[l].astype(BF16), ffn1_w2[l].astype(BF16))
        q, k, vt, xr, gy, gates = _in_proj(x2.reshape(b, s, d), mix_norm[l], w_in[l].astype(BF16), b_gate[l],
                                          q_norm[l], k_norm[l])
        attn = _attention(q, k, vt)
        hf, xc = _rnn_fwd(xr, conv_w[l], conv_b[l], lru_wa[l, 0], lru_ba[l, 0], lru_wx[l, 0], lru_bx[l, 0],
                          lru_lambda[l, 0])
        rg = _rnn_bwd(xc, lru_wa[l, 1], lru_ba[l, 1], lru_wx[l, 1], lru_bx[l, 1], lru_lambda[l, 1], hf, gy)
        x2 = _merge(x2, attn.reshape(b * s, -1), rg.reshape(b * s, -1), gates.reshape(b * s, -1),
                    w_attn_o[l].astype(BF16), w_rnn_o[l].astype(BF16), w_out[l].astype(BF16))
        x2 = _ffn(x2, ffn2_norm[l], ffn2_w1[l].astype(BF16), ffn2_w2[l].astype(BF16))
    return x2.reshape(b, s, d)
```

```python
import functools
import math

import jax
import jax.numpy as jnp
from jax import lax
from jax.experimental import pallas as pl
from jax.experimental.pallas import tpu as pltpu

F32 = jnp.float32
BF16 = jnp.bfloat16

N_HEADS = 16
N_KV_HEADS = 4
HEAD_DIM = 64
GROUP = N_HEADS // N_KV_HEADS
VT_ROWS = HEAD_DIM + 16
SAFE_SHIFT = 60.0
ROPE_THETA = 10000.0
ROPE_QUARTER = HEAD_DIM // 4
GRID_W = 64
N_RNN_BLOCKS = 16
CONV_W = 4
CONV_LEFT = 2
LRU_C = 8.0
FFN_RESID = 0.5
EPS = 1e-6
LOG2E = math.log2(math.e)

LANES = 128
SUBLANES = 8
MXU_DIM = 256
VMEM_LIMIT_BYTES = 56 * 1024 * 1024

FFN_TM = 512
FFN_TF = 256
PROJ_TM = 512
PROJ_ROW_SPLIT = 2
ATT_TQ = 512
ATT_TK = 512
ATT_UNROLL = 2
RNN_TM = 512
RNN_ROW_BLOCK = 256
OUT_TM = 1024


def _params(*sem):
    return pltpu.CompilerParams(dimension_semantics=sem, vmem_limit_bytes=VMEM_LIMIT_BYTES)


def _rms_scale(x):
    return lax.rsqrt(jnp.sum(x * x, axis=-1, keepdims=True) * (1.0 / x.shape[-1]) + EPS)


def _sigmoid(x):
    return 1.0 / (1.0 + jnp.exp2(x * -LOG2E))


def _ffn_kernel(x_ref, g_ref, w1_ref, w2_ref, o_ref):
    d_ff = w2_ref.shape[0]
    x = x_ref[...]
    h = (x * _rms_scale(x) * g_ref[...]).astype(BF16)
    acc = None
    for c in range(0, d_ff, FFN_TF):
        g = jnp.dot(h, w1_ref[:, c:c + FFN_TF], preferred_element_type=F32)
        u = jnp.dot(h, w1_ref[:, d_ff + c:d_ff + c + FFN_TF], preferred_element_type=F32)
        act = (g * _sigmoid(g) * u).astype(BF16)
        part = jnp.dot(act, w2_ref[c:c + FFN_TF, :], preferred_element_type=F32)
        acc = part if acc is None else acc + part
    o_ref[...] = x + FFN_RESID * acc


def _ffn(x2, gain, w1, w2):
    t, d = x2.shape
    tm = min(FFN_TM, t)
    resident = lambda shape: pl.BlockSpec(shape, lambda i: (0, 0), pipeline_mode=pl.Buffered(1))
    return pl.pallas_call(
        _ffn_kernel,
        grid=(t // tm,),
        in_specs=[
            pl.BlockSpec((tm, d), lambda i: (i, 0)),
            pl.BlockSpec((1, d), lambda i: (0, 0)),
            resident(w1.shape),
            resident(w2.shape),
        ],
        out_specs=pl.BlockSpec((tm, d), lambda i: (i, 0)),
        out_shape=jax.ShapeDtypeStruct((t, d), F32),
        compiler_params=_params("parallel"),
        name="ffn",
    )(x2, gain.reshape(1, d), w1, w2)


def _rope_tables(seq):
    pos = jnp.arange(seq, dtype=jnp.int32)
    row = (pos // GRID_W).astype(F32)
    col = (pos % GRID_W).astype(F32)
    half = HEAD_DIM // 2
    inv = ROPE_THETA ** (-jnp.arange(0, half, 2, dtype=F32) / half)
    ang_r = row[:, None] * inv[None, :]
    ang_c = col[:, None] * inv[None, :]
    zero = jnp.zeros_like(ang_r)
    cos = jnp.concatenate([jnp.cos(ang_r), jnp.cos(ang_r), jnp.cos(ang_c), jnp.cos(ang_c)], axis=1)
    s_up = jnp.concatenate([-jnp.sin(ang_r), zero, -jnp.sin(ang_c), zero], axis=1)
    s_dn = jnp.concatenate([zero, jnp.sin(ang_r), zero, jnp.sin(ang_c)], axis=1)
    rep = LANES // HEAD_DIM
    return tuple(jnp.tile(a, (1, rep)) for a in (cos, s_up, s_dn))


def _head_norm_rope(z, seg, gain, cos, s_up, s_dn):
    width = z.shape[1]
    sq = (z * z).astype(BF16)
    ms = jnp.concatenate(
        [jnp.dot(sq[:, c:c + MXU_DIM], seg, preferred_element_type=F32) for c in range(0, width, MXU_DIM)],
        axis=1)
    zn = z * lax.rsqrt(ms + EPS) * gain
    rep = width // LANES
    widen = lambda a: jnp.concatenate([a] * rep, axis=1)
    up = pltpu.roll(zn, width - ROPE_QUARTER, axis=1)
    dn = pltpu.roll(zn, ROPE_QUARTER, axis=1)
    return zn * widen(cos) + up * widen(s_up) + dn * widen(s_dn)


def _proj_kernel(x_ref, gn_ref, w_ref, bg_ref, gq_ref, gk_ref, seg_ref, cos_ref, sup_ref, sdn_ref,
                 qt_ref, k_ref, vt_ref, xr_ref, gy_ref, gate_ref):
    tm = x_ref.shape[1]
    rb = tm // PROJ_ROW_SPLIT
    widths = (N_HEADS * HEAD_DIM, k_ref.shape[1] * HEAD_DIM, k_ref.shape[1] * HEAD_DIM,
              xr_ref.shape[2], gy_ref.shape[2], gate_ref.shape[2])
    starts = [sum(widths[:i]) for i in range(len(widths))]
    wq_ref, wk_ref, wv_ref, wx_ref, wy_ref, wg_ref = (w_ref.at[:, o:o + n] for o, n in zip(starts, widths))
    seg = seg_ref[...]
    qscale = HEAD_DIM ** -0.5 * LOG2E
    for blk in range(PROJ_ROW_SPLIT):
        rows = slice(blk * rb, (blk + 1) * rb)
        x = x_ref[0, rows, :]
        h = (x * _rms_scale(x) * gn_ref[...]).astype(BF16)
        cos, s_up, s_dn = cos_ref[rows, :], sup_ref[rows, :], sdn_ref[rows, :]

        q = jnp.dot(h, wq_ref[...], preferred_element_type=F32)
        q_t = (_head_norm_rope(q, seg, gq_ref[...], cos, s_up, s_dn) * qscale).T.astype(BF16)
        for j in range(N_KV_HEADS):
            for g in range(GROUP):
                r0 = (j * GROUP + g) * HEAD_DIM
                qt_ref[0, 0, j, :, g * tm + blk * rb:g * tm + (blk + 1) * rb] = q_t[r0:r0 + HEAD_DIM, :]

        k = jnp.dot(h, wk_ref[...], preferred_element_type=F32)
        k = _head_norm_rope(k, seg, gk_ref[...], cos, s_up, s_dn).astype(BF16)
        vt = jnp.dot(h, wv_ref[...], preferred_element_type=F32).T.astype(BF16)
        for j in range(N_KV_HEADS):
            k_ref[0, j, rows, :] = k[:, j * HEAD_DIM:(j + 1) * HEAD_DIM]
            vt_ref[0, j, 0, 0:HEAD_DIM, rows] = vt[j * HEAD_DIM:(j + 1) * HEAD_DIM, :]
            vt_ref[0, j, 0, HEAD_DIM:VT_ROWS, rows] = jnp.ones((VT_ROWS - HEAD_DIM, rb), BF16)

        xr_ref[0, rows, :] = jnp.dot(h, wx_ref[...], preferred_element_type=F32)

        y = jnp.dot(h, wy_ref[...], preferred_element_type=F32)
        gelu = 0.5 * y * (1.0 + jnp.tanh(math.sqrt(2.0 / math.pi) * (y + 0.044715 * (y * y * y))))
        gy_ref[0, rows, :] = gelu.astype(BF16)

        gl = jnp.dot(h, wg_ref[...], preferred_element_type=F32) + bg_ref[...]
        gate_ref[0, rows, :] = _sigmoid(gl).astype(BF16)


def _in_proj(x, gain, w_in, b_gate, q_gain, k_gain):
    b, s, d = x.shape
    q_cols = N_HEADS * HEAD_DIM
    kv_cols = N_KV_HEADS * HEAD_DIM
    d_rnn = (w_in.shape[1] - q_cols - 2 * kv_cols - 2 * d) // 2
    tm = min(PROJ_TM, s)
    nt = s // tm
    cos, s_up, s_dn = _rope_tables(s)
    lane = jnp.arange(MXU_DIM)
    seg = jnp.where(lane[:, None] // HEAD_DIM == lane[None, :] // HEAD_DIM, 1.0 / HEAD_DIM, 0.0).astype(BF16)
    const = lambda shape: pl.BlockSpec(shape, lambda bi, ti: (0,) * len(shape))
    tab = pl.BlockSpec((tm, LANES), lambda bi, ti: (ti, 0))
    row = lambda width: pl.BlockSpec((1, tm, width), lambda bi, ti: (bi, ti, 0))
    head = pl.BlockSpec((1, N_KV_HEADS, tm, HEAD_DIM), lambda bi, ti: (bi, 0, ti, 0))
    head_t = pl.BlockSpec((1, N_KV_HEADS, 1, VT_ROWS, tm), lambda bi, ti: (bi, 0, ti, 0, 0))
    return pl.pallas_call(
        _proj_kernel,
        grid=(b, nt),
        in_specs=[row(d), const((1, d)),
                  pl.BlockSpec(w_in.shape, lambda bi, ti: (0, 0), pipeline_mode=pl.Buffered(1)),
                  const((1, 2 * d)), const((1, q_cols)), const((1, kv_cols)), const((MXU_DIM, MXU_DIM)), tab, tab, tab],
        out_specs=[pl.BlockSpec((1, 1, N_KV_HEADS, HEAD_DIM, GROUP * tm), lambda bi, ti: (bi, ti, 0, 0, 0)),
                   head, head_t, row(d_rnn), row(d_rnn), row(2 * d)],
        out_shape=[
            jax.ShapeDtypeStruct((b, nt, N_KV_HEADS, HEAD_DIM, GROUP * tm), BF16),
            jax.ShapeDtypeStruct((b, N_KV_HEADS, s, HEAD_DIM), BF16),
            jax.ShapeDtypeStruct((b, N_KV_HEADS, nt, VT_ROWS, tm), BF16),
            jax.ShapeDtypeStruct((b, s, d_rnn), F32),
            jax.ShapeDtypeStruct((b, s, d_rnn), BF16),
            jax.ShapeDtypeStruct((b, s, 2 * d), BF16),
        ],
        compiler_params=_params("parallel", "parallel"),
        name="in_proj",
    )(x, gain.reshape(1, d), w_in, b_gate.reshape(1, 2 * d),
      jnp.tile(q_gain, N_HEADS).reshape(1, q_cols), jnp.tile(k_gain, N_KV_HEADS).reshape(1, kv_cols),
      seg, cos, s_up, s_dn)


def _attn_kernel(q_ref, k_ref, vt_ref, o_ref, shift_ref, m_ref, acc_ref, kmax_ref, *, tk):
    qt = q_ref.at[0, 0, 0]
    seq = k_ref.shape[2]
    kc = vt_ref.shape[4]
    nsub = tk // kc
    nchunk = seq // tk

    @pl.when(pl.program_id(2) == 0)
    def _():
        def norm_chunk(c, best):
            kk = k_ref[0, 0, pl.ds(pl.multiple_of(c * tk, tk), tk), :].astype(F32)
            return jnp.maximum(best, jnp.max(jnp.sum(kk * kk, axis=1, keepdims=True), axis=0, keepdims=True))
        kmax_ref[...] = jnp.sqrt(lax.fori_loop(0, nchunk, norm_chunk, jnp.zeros((1, 1), F32)))

    qf = qt[...].astype(F32)
    shift_ref[...] = jnp.sqrt(jnp.sum(qf * qf, axis=0, keepdims=True)) * kmax_ref[...]
    acc_ref[...] = jnp.zeros_like(acc_ref)
    safe = jnp.max(shift_ref[...]) <= SAFE_SHIFT

    def pv(c, p_t):
        vt = jnp.concatenate([vt_ref[0, 0, c * nsub + sub] for sub in range(nsub)], axis=1)
        return jnp.dot(vt, p_t, preferred_element_type=F32)

    def scores(c):
        k = k_ref[0, 0, pl.ds(pl.multiple_of(c * tk, tk), tk), :]
        return jnp.dot(k, qt[...], preferred_element_type=F32)

    @pl.when(safe)
    def _():
        def body(c, carry):
            p_t = jnp.exp2(scores(c) - shift_ref[...]).astype(BF16)
            acc_ref[...] += pv(c, p_t)
            return carry
        lax.fori_loop(0, nchunk, body, 0, unroll=min(ATT_UNROLL, nchunk))

    @pl.when(jnp.logical_not(safe))
    def _():
        m_ref[...] = jnp.full_like(m_ref, -jnp.inf)

        def body(c, carry):
            s_t = scores(c)
            m_prev = m_ref[...]
            m_new = jnp.maximum(m_prev, jnp.max(s_t, axis=0, keepdims=True))
            p_t = jnp.exp2(s_t - m_new).astype(BF16)
            acc_ref[...] = jnp.exp2(m_prev - m_new) * acc_ref[...] + pv(c, p_t)
            m_ref[...] = m_new
            return carry
        lax.fori_loop(0, nchunk, body, 0)

    o_ref[0, 0, 0] = (acc_ref[0:HEAD_DIM, :] / acc_ref[HEAD_DIM:HEAD_DIM + 1, :]).astype(o_ref.dtype)


def _attention(qt, k, vt):
    b, nq, _, _, gtq = qt.shape
    s = k.shape[2]
    tk = min(ATT_TK, s)
    nchunk, vt_rows, kc = vt.shape[2:]
    tile = pl.BlockSpec((1, 1, 1, HEAD_DIM, gtq), lambda bi, j, i: (bi, i, j, 0, 0))
    return pl.pallas_call(
        functools.partial(_attn_kernel, tk=tk),
        grid=(b, N_KV_HEADS, nq),
        in_specs=[
            tile,
            pl.BlockSpec((1, 1, s, HEAD_DIM), lambda bi, j, i: (bi, j, 0, 0)),
            pl.BlockSpec((1, 1, nchunk, vt_rows, kc), lambda bi, j, i: (bi, j, 0, 0, 0)),
        ],
        out_specs=tile,
        out_shape=jax.ShapeDtypeStruct(qt.shape, BF16),
        scratch_shapes=[
            pltpu.VMEM((1, gtq), F32),
            pltpu.VMEM((1, gtq), F32),
            pltpu.VMEM((vt_rows, gtq), F32),
            pltpu.VMEM((1, 1), F32),
        ],
        compiler_params=_params("parallel", "arbitrary", "arbitrary"),
        name="attention",
    )(qt, k, vt)


def _lru_gates_and_scan(xc, wcat_ref, ba_ref, bx_ref, lam_ref, a_ref, u_ref, hs_ref, hc_ref, reverse):
    tm, d = xc.shape
    rb = min(RNN_ROW_BLOCK, tm)
    lam = lam_ref[...]
    softplus_neg_lam = jnp.maximum(-lam, 0.0) + jnp.log1p(jnp.exp(-jnp.abs(lam)))
    decay_rate = (-LRU_C * LOG2E) * softplus_neg_lam
    h = hc_ref[...]
    order = range(tm // rb - 1, -1, -1) if reverse else range(tm // rb)
    for blk in order:
        rows = slice(blk * rb, (blk + 1) * rb)
        xb = xc[rows, :]
        xbb = xb.astype(BF16)
        for c in range(d // MXU_DIM):
            sl = slice(c * MXU_DIM, (c + 1) * MXU_DIM)
            g = jnp.dot(xbb[:, sl], wcat_ref[c], preferred_element_type=F32)
            r = _sigmoid(g[:, :MXU_DIM] + ba_ref[:, sl])
            i = _sigmoid(g[:, MXU_DIM:] + bx_ref[:, sl])
            a = jnp.exp2(r * decay_rate[:, sl])
            a_ref[rows, sl] = a
            y = 1.0 - a * a
            u_ref[rows, sl] = y * lax.rsqrt(jnp.maximum(y, 1e-30)) * (i * xb[:, sl])
        for row in (range((blk + 1) * rb - 1, blk * rb - 1, -1) if reverse else range(blk * rb, (blk + 1) * rb)):
            h = a_ref[row:row + 1, :] * h + u_ref[row:row + 1, :]
            hs_ref[row:row + 1, :] = h
    hc_ref[...] = h


def _rnn_fwd_kernel(xr_ref, prev_ref, next_ref, cw_ref, cb_ref, wcat_ref, ba_ref, bx_ref, lam_ref,
                    hf_ref, xc_ref, xpad_ref, a_ref, u_ref, hc_ref):
    t = pl.program_id(1)
    nt = pl.num_programs(1)
    tm = xr_ref.shape[1]
    halo = SUBLANES

    @pl.when(t == 0)
    def _():
        hc_ref[...] = jnp.zeros_like(hc_ref)

    xpad_ref[halo:halo + tm, :] = xr_ref[0]
    xpad_ref[0:halo, :] = jnp.where(t > 0, prev_ref[0], 0.0)
    xpad_ref[halo + tm:halo + tm + halo, :] = jnp.where(t < nt - 1, next_ref[0], 0.0)
    xc = cb_ref[...] + sum(
        cw_ref[kk:kk + 1, :] * xpad_ref[halo - CONV_LEFT + kk:halo - CONV_LEFT + kk + tm, :]
        for kk in range(CONV_W))
    xc_ref[0] = xc
    _lru_gates_and_scan(xc, wcat_ref, ba_ref, bx_ref, lam_ref, a_ref, u_ref, hf_ref.at[0], hc_ref, reverse=False)


def _rnn_bwd_kernel(xc_ref, wcat_ref, ba_ref, bx_ref, lam_ref, hf_ref, gy_ref, o_ref, a_ref, u_ref, hs_ref, hc_ref):
    @pl.when(pl.program_id(1) == 0)
    def _():
        hc_ref[...] = jnp.zeros_like(hc_ref)

    _lru_gates_and_scan(xc_ref[0], wcat_ref, ba_ref, bx_ref, lam_ref, a_ref, u_ref, hs_ref, hc_ref, reverse=True)
    o_ref[0] = ((hf_ref[0] + hs_ref[...]) * gy_ref[0].astype(F32)).astype(o_ref.dtype)


def _gate_slabs(wa, wx, d):
    nb, blk, _ = wa.shape
    per = MXU_DIM // blk
    nslab = d // MXU_DIM

    def slab_diag(w):
        w = w.reshape(nslab, per, blk, blk)
        eye = jnp.eye(per, dtype=w.dtype)
        return jnp.einsum('spij,pq->spiqj', w, eye).reshape(nslab, MXU_DIM, MXU_DIM)

    return jnp.concatenate([slab_diag(wa), slab_diag(wx)], axis=2).astype(BF16)


def _rnn_fwd(xr, conv_w, conv_b, wa, ba, wx, bx, lam):
    b, s, d = xr.shape
    tm = min(RNN_TM, s)
    nt = s // tm
    hb = tm // SUBLANES
    wcat = _gate_slabs(wa, wx, d)
    main = pl.BlockSpec((1, tm, d), lambda bi, t: (bi, t, 0))
    const = lambda shape: pl.BlockSpec(shape, lambda bi, t: (0,) * len(shape))
    return pl.pallas_call(
        _rnn_fwd_kernel,
        grid=(b, nt),
        in_specs=[
            main,
            pl.BlockSpec((1, SUBLANES, d), lambda bi, t: (bi, jnp.maximum(t * hb - 1, 0), 0)),
            pl.BlockSpec((1, SUBLANES, d), lambda bi, t: (bi, jnp.minimum((t + 1) * hb, s // SUBLANES - 1), 0)),
            const((CONV_W, d)), const((1, d)), const(wcat.shape), const((1, d)), const((1, d)), const((1, d)),
        ],
        out_specs=[main, main],
        out_shape=[jax.ShapeDtypeStruct((b, s, d), F32), jax.ShapeDtypeStruct((b, s, d), F32)],
        scratch_shapes=[
            pltpu.VMEM((tm + 2 * SUBLANES, d), F32),
            pltpu.VMEM((tm, d), F32),
            pltpu.VMEM((tm, d), F32),
            pltpu.VMEM((1, d), F32),
        ],
        compiler_params=_params("parallel", "arbitrary"),
        name="rnn_fwd",
    )(xr, xr, xr, conv_w, conv_b.reshape(1, d), wcat, ba.reshape(1, d), bx.reshape(1, d), lam.reshape(1, d))


def _rnn_bwd(xc, wa, ba, wx, bx, lam, hf, gy):
    b, s, d = xc.shape
    tm = min(RNN_TM, s)
    nt = s // tm
    wcat = _gate_slabs(wa, wx, d)
    main = pl.BlockSpec((1, tm, d), lambda bi, t: (bi, nt - 1 - t, 0))
    const = lambda shape: pl.BlockSpec(shape, lambda bi, t: (0,) * len(shape))
    return pl.pallas_call(
        _rnn_bwd_kernel,
        grid=(b, nt),
        in_specs=[main, const(wcat.shape), const((1, d)), const((1, d)), const((1, d)), main, main],
        out_specs=main,
        out_shape=jax.ShapeDtypeStruct((b, s, d), BF16),
        scratch_shapes=[
            pltpu.VMEM((tm, d), F32),
            pltpu.VMEM((tm, d), F32),
            pltpu.VMEM((tm, d), F32),
            pltpu.VMEM((1, d), F32),
        ],
        compiler_params=_params("parallel", "arbitrary"),
        name="rnn_bwd",
    )(xc, wcat, ba.reshape(1, d), bx.reshape(1, d), lam.reshape(1, d), hf, gy)


def _merge_kernel(x_ref, at_ref, rg_ref, gate_ref, wa_ref, wr_ref, wo_ref, o_ref):
    d = x_ref.shape[1]
    n_tiles, n_kv, hd, gtq = at_ref.shape
    tq = gtq // GROUP
    slab = GROUP * hd
    a_tiles = []
    for t in range(n_tiles):
        a = None
        for j in range(n_kv):
            o_t = at_ref[t, j].astype(F32)
            lhs = jnp.concatenate([o_t[:, g * tq:(g + 1) * tq] for g in range(GROUP)], axis=0).T.astype(BF16)
            term = jnp.dot(lhs, wa_ref[j * slab:(j + 1) * slab, :], preferred_element_type=F32)
            a = term if a is None else a + term
        a_tiles.append(a)
    a = jnp.concatenate(a_tiles, axis=0)
    r = jnp.dot(rg_ref[...], wr_ref[...], preferred_element_type=F32)
    merged = gate_ref[:, :d].astype(F32) * a + gate_ref[:, d:].astype(F32) * r
    o_ref[...] = x_ref[...] + jnp.dot(merged.astype(BF16), wo_ref[...], preferred_element_type=F32)


def _merge(x2, attn_t, rg2, gate2, w_attn_o, w_rnn_o, w_out):
    t, d = x2.shape
    tq = attn_t.shape[4] // GROUP
    tm = min(OUT_TM, t)
    n_tiles = tm // tq
    at = attn_t.reshape((-1,) + attn_t.shape[2:])
    rows = lambda width: pl.BlockSpec((tm, width), lambda i: (i, 0))
    const = lambda shape: pl.BlockSpec(shape, lambda i: (0, 0))
    return pl.pallas_call(
        _merge_kernel,
        grid=(t // tm,),
        in_specs=[rows(d), pl.BlockSpec((n_tiles,) + at.shape[1:], lambda i: (i, 0, 0, 0)),
                  rows(rg2.shape[1]), rows(2 * d),
                  const(w_attn_o.shape), const(w_rnn_o.shape), const(w_out.shape)],
        out_specs=rows(d),
        out_shape=jax.ShapeDtypeStruct((t, d), F32),
        compiler_params=_params("parallel"),
        name="merge",
    )(x2, at, rg2, gate2, w_attn_o, w_rnn_o, w_out)


def kernel(x, ffn1_norm, ffn1_w1, ffn1_w2, mix_norm, w_in, b_gate, q_norm, k_norm, w_attn_o, conv_w, conv_b,
           lru_wa, lru_ba, lru_wx, lru_bx, lru_lambda, w_rnn_o, w_out, ffn2_norm, ffn2_w1, ffn2_w2):
    b, s, d = x.shape
    depth = w_in.shape[0]
    x2 = x.reshape(b * s, d)
    for l in range(depth):
        x2 = _ffn(x2, ffn1_norm[l], ffn1_w1[l].astype(BF16), ffn1_w2[l].astype(BF16))
        q, k, vt, xr, gy, gates = _in_proj(x2.reshape(b, s, d), mix_norm[l], w_in[l].astype(BF16), b_gate[l],
                                          q_norm[l], k_norm[l])
        attn = _attention(q, k, vt)
        hf, xc = _rnn_fwd(xr, conv_w[l], conv_b[l], lru_wa[l, 0], lru_ba[l, 0], lru_wx[l, 0], lru_bx[l, 0],
                          lru_lambda[l, 0])
        rg = _rnn_bwd(xc, lru_wa[l, 1], lru_ba[l, 1], lru_wx[l, 1], lru_bx[l, 1], lru_lambda[l, 1], hf, gy)
        x2 = _merge(x2, attn, rg.reshape(b * s, -1), gates.reshape(b * s, -1),
                    w_attn_o[l].astype(BF16), w_rnn_o[l].astype(BF16), w_out[l].astype(BF16))
        x2 = _ffn(x2, ffn2_norm[l], ffn2_w1[l].astype(BF16), ffn2_w2[l].astype(BF16))
    return x2.reshape(b, s, d)
```

```python
import functools
import math

import jax
import jax.numpy as jnp
from jax import lax
from jax.experimental import pallas as pl
from jax.experimental.pallas import tpu as pltpu

F32 = jnp.float32
BF16 = jnp.bfloat16

N_HEADS = 16
N_KV_HEADS = 4
HEAD_DIM = 64
GROUP = N_HEADS // N_KV_HEADS
VT_ROWS = HEAD_DIM + 16
SAFE_SHIFT = 60.0
ROPE_THETA = 10000.0
ROPE_QUARTER = HEAD_DIM // 4
GRID_W = 64
N_RNN_BLOCKS = 16
CONV_W = 4
CONV_LEFT = 2
LRU_C = 8.0
FFN_RESID = 0.5
EPS = 1e-6
LOG2E = math.log2(math.e)

LANES = 128
SUBLANES = 8
MXU_DIM = 256
VMEM_LIMIT_BYTES = 56 * 1024 * 1024

FFN_TM = 512
FFN_TF = 256
PROJ_TM = 512
PROJ_ROW_SPLIT = 2
ATT_TQ = 512
ATT_TK = 512
ATT_UNROLL = 2
RNN_TM = 512
RNN_ROW_BLOCK = 256
OUT_TM = 1024


def _params(*sem):
    return pltpu.CompilerParams(dimension_semantics=sem, vmem_limit_bytes=VMEM_LIMIT_BYTES)


def _rms_scale(x):
    return lax.rsqrt(jnp.sum(x * x, axis=-1, keepdims=True) * (1.0 / x.shape[-1]) + EPS)


def _sigmoid(x):
    return 1.0 / (1.0 + jnp.exp2(x * -LOG2E))


def _ffn_kernel(x_ref, g_ref, w1_ref, w2_ref, o_ref):
    d_ff = w2_ref.shape[0]
    x = x_ref[...]
    h = (x * _rms_scale(x) * g_ref[...]).astype(BF16)
    acc = None
    for c in range(0, d_ff, FFN_TF):
        g = jnp.dot(h, w1_ref[:, c:c + FFN_TF], preferred_element_type=F32)
        u = jnp.dot(h, w1_ref[:, d_ff + c:d_ff + c + FFN_TF], preferred_element_type=F32)
        act = (g * _sigmoid(g) * u).astype(BF16)
        part = jnp.dot(act, w2_ref[c:c + FFN_TF, :], preferred_element_type=F32)
        acc = part if acc is None else acc + part
    o_ref[...] = x + FFN_RESID * acc


def _ffn(x2, gain, w1, w2, layer):
    t, d = x2.shape
    tm = min(FFN_TM, t)
    resident = lambda shape: pl.BlockSpec((None,) + shape[1:], lambda i: (layer, 0, 0),
                                          pipeline_mode=pl.Buffered(1))
    return pl.pallas_call(
        _ffn_kernel,
        grid=(t // tm,),
        in_specs=[
            pl.BlockSpec((tm, d), lambda i: (i, 0)),
            pl.BlockSpec((1, d), lambda i: (0, 0)),
            resident(w1.shape),
            resident(w2.shape),
        ],
        out_specs=pl.BlockSpec((tm, d), lambda i: (i, 0)),
        out_shape=jax.ShapeDtypeStruct((t, d), F32),
        compiler_params=_params("parallel"),
        name="ffn",
    )(x2, gain.reshape(1, d), w1, w2)


def _rope_tables(seq):
    pos = jnp.arange(seq, dtype=jnp.int32)
    row = (pos // GRID_W).astype(F32)
    col = (pos % GRID_W).astype(F32)
    half = HEAD_DIM // 2
    inv = ROPE_THETA ** (-jnp.arange(0, half, 2, dtype=F32) / half)
    ang_r = row[:, None] * inv[None, :]
    ang_c = col[:, None] * inv[None, :]
    zero = jnp.zeros_like(ang_r)
    cos = jnp.concatenate([jnp.cos(ang_r), jnp.cos(ang_r), jnp.cos(ang_c), jnp.cos(ang_c)], axis=1)
    s_up = jnp.concatenate([-jnp.sin(ang_r), zero, -jnp.sin(ang_c), zero], axis=1)
    s_dn = jnp.concatenate([zero, jnp.sin(ang_r), zero, jnp.sin(ang_c)], axis=1)
    rep = LANES // HEAD_DIM
    return tuple(jnp.tile(a, (1, rep)) for a in (cos, s_up, s_dn))


def _head_norm_rope(z, seg, gain, cos, s_up, s_dn):
    width = z.shape[1]
    sq = (z * z).astype(BF16)
    ms = jnp.concatenate(
        [jnp.dot(sq[:, c:c + MXU_DIM], seg, preferred_element_type=F32) for c in range(0, width, MXU_DIM)],
        axis=1)
    zn = z * lax.rsqrt(ms + EPS) * gain
    rep = width // LANES
    widen = lambda a: jnp.concatenate([a] * rep, axis=1)
    up = pltpu.roll(zn, width - ROPE_QUARTER, axis=1)
    dn = pltpu.roll(zn, ROPE_QUARTER, axis=1)
    return zn * widen(cos) + up * widen(s_up) + dn * widen(s_dn)


def _proj_kernel(x_ref, gn_ref, w_ref, bg_ref, gq_ref, gk_ref, seg_ref, cos_ref, sup_ref, sdn_ref,
                 qt_ref, k_ref, vt_ref, xr_ref, gy_ref, gate_ref):
    tm = x_ref.shape[1]
    rb = tm // PROJ_ROW_SPLIT
    widths = (N_HEADS * HEAD_DIM, k_ref.shape[1] * HEAD_DIM, k_ref.shape[1] * HEAD_DIM,
              xr_ref.shape[2], gy_ref.shape[2], gate_ref.shape[2])
    starts = [sum(widths[:i]) for i in range(len(widths))]
    wq_ref, wk_ref, wv_ref, wx_ref, wy_ref, wg_ref = (w_ref.at[:, o:o + n] for o, n in zip(starts, widths))
    seg = seg_ref[...]
    qscale = HEAD_DIM ** -0.5 * LOG2E
    for blk in range(PROJ_ROW_SPLIT):
        rows = slice(blk * rb, (blk + 1) * rb)
        x = x_ref[0, rows, :]
        h = (x * _rms_scale(x) * gn_ref[...]).astype(BF16)
        cos, s_up, s_dn = cos_ref[rows, :], sup_ref[rows, :], sdn_ref[rows, :]

        q = jnp.dot(h, wq_ref[...], preferred_element_type=F32)
        q_t = (_head_norm_rope(q, seg, gq_ref[...], cos, s_up, s_dn) * qscale).T.astype(BF16)
        for j in range(N_KV_HEADS):
            for g in range(GROUP):
                r0 = (j * GROUP + g) * HEAD_DIM
                qt_ref[0, 0, j, :, g * tm + blk * rb:g * tm + (blk + 1) * rb] = q_t[r0:r0 + HEAD_DIM, :]

        k = jnp.dot(h, wk_ref[...], preferred_element_type=F32)
        k = _head_norm_rope(k, seg, gk_ref[...], cos, s_up, s_dn).astype(BF16)
        vt = jnp.dot(h, wv_ref[...], preferred_element_type=F32).T.astype(BF16)
        for j in range(N_KV_HEADS):
            k_ref[0, j, rows, :] = k[:, j * HEAD_DIM:(j + 1) * HEAD_DIM]
            vt_ref[0, j, 0, 0:HEAD_DIM, rows] = vt[j * HEAD_DIM:(j + 1) * HEAD_DIM, :]
            vt_ref[0, j, 0, HEAD_DIM:VT_ROWS, rows] = jnp.ones((VT_ROWS - HEAD_DIM, rb), BF16)

        xr_ref[0, rows, :] = jnp.dot(h, wx_ref[...], preferred_element_type=F32)

        y = jnp.dot(h, wy_ref[...], preferred_element_type=F32)
        gelu = 0.5 * y * (1.0 + jnp.tanh(math.sqrt(2.0 / math.pi) * (y + 0.044715 * (y * y * y))))
        gy_ref[0, rows, :] = gelu.astype(BF16)

        gl = jnp.dot(h, wg_ref[...], preferred_element_type=F32) + bg_ref[...]
        gate_ref[0, rows, :] = _sigmoid(gl).astype(BF16)


def _in_proj(x, gain, w_in, layer, b_gate, q_gain, k_gain):
    b, s, d = x.shape
    q_cols = N_HEADS * HEAD_DIM
    kv_cols = N_KV_HEADS * HEAD_DIM
    d_rnn = (w_in.shape[2] - q_cols - 2 * kv_cols - 2 * d) // 2
    tm = min(PROJ_TM, s)
    nt = s // tm
    cos, s_up, s_dn = _rope_tables(s)
    lane = jnp.arange(MXU_DIM)
    seg = jnp.where(lane[:, None] // HEAD_DIM == lane[None, :] // HEAD_DIM, 1.0 / HEAD_DIM, 0.0).astype(BF16)
    const = lambda shape: pl.BlockSpec(shape, lambda bi, ti: (0,) * len(shape))
    tab = pl.BlockSpec((tm, LANES), lambda bi, ti: (ti, 0))
    row = lambda width: pl.BlockSpec((1, tm, width), lambda bi, ti: (bi, ti, 0))
    head = pl.BlockSpec((1, N_KV_HEADS, tm, HEAD_DIM), lambda bi, ti: (bi, 0, ti, 0))
    head_t = pl.BlockSpec((1, N_KV_HEADS, 1, VT_ROWS, tm), lambda bi, ti: (bi, 0, ti, 0, 0))
    return pl.pallas_call(
        _proj_kernel,
        grid=(b, nt),
        in_specs=[row(d), const((1, d)),
                  pl.BlockSpec((None,) + w_in.shape[1:], lambda bi, ti: (layer, 0, 0), pipeline_mode=pl.Buffered(1)),
                  const((1, 2 * d)), const((1, q_cols)), const((1, kv_cols)), const((MXU_DIM, MXU_DIM)), tab, tab, tab],
        out_specs=[pl.BlockSpec((1, 1, N_KV_HEADS, HEAD_DIM, GROUP * tm), lambda bi, ti: (bi, ti, 0, 0, 0)),
                   head, head_t, row(d_rnn), row(d_rnn), row(2 * d)],
        out_shape=[
            jax.ShapeDtypeStruct((b, nt, N_KV_HEADS, HEAD_DIM, GROUP * tm), BF16),
            jax.ShapeDtypeStruct((b, N_KV_HEADS, s, HEAD_DIM), BF16),
            jax.ShapeDtypeStruct((b, N_KV_HEADS, nt, VT_ROWS, tm), BF16),
            jax.ShapeDtypeStruct((b, s, d_rnn), F32),
            jax.ShapeDtypeStruct((b, s, d_rnn), BF16),
            jax.ShapeDtypeStruct((b, s, 2 * d), BF16),
        ],
        compiler_params=_params("parallel", "parallel"),
        name="in_proj",
    )(x, gain.reshape(1, d), w_in, b_gate.reshape(1, 2 * d),
      jnp.tile(q_gain, N_HEADS).reshape(1, q_cols), jnp.tile(k_gain, N_KV_HEADS).reshape(1, kv_cols),
      seg, cos, s_up, s_dn)


def _attn_kernel(q_ref, k_ref, vt_ref, o_ref, shift_ref, m_ref, acc_ref, kmax_ref, *, tk):
    qt = q_ref.at[0, 0, 0]
    seq = k_ref.shape[2]
    kc = vt_ref.shape[4]
    nsub = tk // kc
    nchunk = seq // tk

    @pl.when(pl.program_id(2) == 0)
    def _():
        def norm_chunk(c, best):
            kk = k_ref[0, 0, pl.ds(pl.multiple_of(c * tk, tk), tk), :].astype(F32)
            return jnp.maximum(best, jnp.max(jnp.sum(kk * kk, axis=1, keepdims=True), axis=0, keepdims=True))
        kmax_ref[...] = jnp.sqrt(lax.fori_loop(0, nchunk, norm_chunk, jnp.zeros((1, 1), F32)))

    qf = qt[...].astype(F32)
    shift_ref[...] = jnp.sqrt(jnp.sum(qf * qf, axis=0, keepdims=True)) * kmax_ref[...]
    acc_ref[...] = jnp.zeros_like(acc_ref)
    safe = jnp.max(shift_ref[...]) <= SAFE_SHIFT

    def pv(c, p_t):
        vt = jnp.concatenate([vt_ref[0, 0, c * nsub + sub] for sub in range(nsub)], axis=1)
        return jnp.dot(vt, p_t, preferred_element_type=F32)

    def scores(c):
        k = k_ref[0, 0, pl.ds(pl.multiple_of(c * tk, tk), tk), :]
        return jnp.dot(k, qt[...], preferred_element_type=F32)

    @pl.when(safe)
    def _():
        def body(c, carry):
            p_t = jnp.exp2(scores(c) - shift_ref[...]).astype(BF16)
            acc_ref[...] += pv(c, p_t)
            return carry
        lax.fori_loop(0, nchunk, body, 0, unroll=min(ATT_UNROLL, nchunk))

    @pl.when(jnp.logical_not(safe))
    def _():
        m_ref[...] = jnp.full_like(m_ref, -jnp.inf)

        def body(c, carry):
            s_t = scores(c)
            m_prev = m_ref[...]
            m_new = jnp.maximum(m_prev, jnp.max(s_t, axis=0, keepdims=True))
            p_t = jnp.exp2(s_t - m_new).astype(BF16)
            acc_ref[...] = jnp.exp2(m_prev - m_new) * acc_ref[...] + pv(c, p_t)
            m_ref[...] = m_new
            return carry
        lax.fori_loop(0, nchunk, body, 0)

    o_ref[0, 0, 0] = (acc_ref[0:HEAD_DIM, :] / acc_ref[HEAD_DIM:HEAD_DIM + 1, :]).astype(o_ref.dtype)


def _attention(qt, k, vt):
    b, nq, _, _, gtq = qt.shape
    s = k.shape[2]
    tk = min(ATT_TK, s)
    nchunk, vt_rows, kc = vt.shape[2:]
    tile = pl.BlockSpec((1, 1, 1, HEAD_DIM, gtq), lambda bi, j, i: (bi, i, j, 0, 0))
    return pl.pallas_call(
        functools.partial(_attn_kernel, tk=tk),
        grid=(b, N_KV_HEADS, nq),
        in_specs=[
            tile,
            pl.BlockSpec((1, 1, s, HEAD_DIM), lambda bi, j, i: (bi, j, 0, 0)),
            pl.BlockSpec((1, 1, nchunk, vt_rows, kc), lambda bi, j, i: (bi, j, 0, 0, 0)),
        ],
        out_specs=tile,
        out_shape=jax.ShapeDtypeStruct(qt.shape, BF16),
        scratch_shapes=[
            pltpu.VMEM((1, gtq), F32),
            pltpu.VMEM((1, gtq), F32),
            pltpu.VMEM((vt_rows, gtq), F32),
            pltpu.VMEM((1, 1), F32),
        ],
        compiler_params=_params("parallel", "arbitrary", "arbitrary"),
        name="attention",
    )(qt, k, vt)


def _lru_gates_and_scan(xc, wcat_ref, ba_ref, bx_ref, lam_ref, a_ref, u_ref, hs_ref, hc_ref, reverse):
    tm, d = xc.shape
    rb = min(RNN_ROW_BLOCK, tm)
    lam = lam_ref[...]
    softplus_neg_lam = jnp.maximum(-lam, 0.0) + jnp.log1p(jnp.exp(-jnp.abs(lam)))
    decay_rate = (-LRU_C * LOG2E) * softplus_neg_lam
    h = hc_ref[...]
    order = range(tm // rb - 1, -1, -1) if reverse else range(tm // rb)
    for blk in order:
        rows = slice(blk * rb, (blk + 1) * rb)
        xb = xc[rows, :]
        xbb = xb.astype(BF16)
        for c in range(d // MXU_DIM):
            sl = slice(c * MXU_DIM, (c + 1) * MXU_DIM)
            g = jnp.dot(xbb[:, sl], wcat_ref[c], preferred_element_type=F32)
            r = _sigmoid(g[:, :MXU_DIM] + ba_ref[:, sl])
            i = _sigmoid(g[:, MXU_DIM:] + bx_ref[:, sl])
            a = jnp.exp2(r * decay_rate[:, sl])
            a_ref[rows, sl] = a
            y = 1.0 - a * a
            u_ref[rows, sl] = y * lax.rsqrt(jnp.maximum(y, 1e-30)) * (i * xb[:, sl])
        for row in (range((blk + 1) * rb - 1, blk * rb - 1, -1) if reverse else range(blk * rb, (blk + 1) * rb)):
            h = a_ref[row:row + 1, :] * h + u_ref[row:row + 1, :]
            hs_ref[row:row + 1, :] = h
    hc_ref[...] = h


def _rnn_fwd_kernel(xr_ref, prev_ref, next_ref, cw_ref, cb_ref, wcat_ref, ba_ref, bx_ref, lam_ref,
                    hf_ref, xc_ref, xpad_ref, a_ref, u_ref, hc_ref):
    t = pl.program_id(1)
    nt = pl.num_programs(1)
    tm = xr_ref.shape[1]
    halo = SUBLANES

    @pl.when(t == 0)
    def _():
        hc_ref[...] = jnp.zeros_like(hc_ref)

    xpad_ref[halo:halo + tm, :] = xr_ref[0]
    xpad_ref[0:halo, :] = jnp.where(t > 0, prev_ref[0], 0.0)
    xpad_ref[halo + tm:halo + tm + halo, :] = jnp.where(t < nt - 1, next_ref[0], 0.0)
    xc = cb_ref[...] + sum(
        cw_ref[kk:kk + 1, :] * xpad_ref[halo - CONV_LEFT + kk:halo - CONV_LEFT + kk + tm, :]
        for kk in range(CONV_W))
    xc_ref[0] = xc
    _lru_gates_and_scan(xc, wcat_ref, ba_ref, bx_ref, lam_ref, a_ref, u_ref, hf_ref.at[0], hc_ref, reverse=False)


def _rnn_bwd_kernel(xc_ref, wcat_ref, ba_ref, bx_ref, lam_ref, hf_ref, gy_ref, o_ref, a_ref, u_ref, hs_ref, hc_ref):
    @pl.when(pl.program_id(1) == 0)
    def _():
        hc_ref[...] = jnp.zeros_like(hc_ref)

    _lru_gates_and_scan(xc_ref[0], wcat_ref, ba_ref, bx_ref, lam_ref, a_ref, u_ref, hs_ref, hc_ref, reverse=True)
    o_ref[0] = ((hf_ref[0] + hs_ref[...]) * gy_ref[0].astype(F32)).astype(o_ref.dtype)


def _gate_slabs(wa, wx, d):
    nb, blk, _ = wa.shape
    per = MXU_DIM // blk
    nslab = d // MXU_DIM

    def slab_diag(w):
        w = w.reshape(nslab, per, blk, blk)
        eye = jnp.eye(per, dtype=w.dtype)
        return jnp.einsum('spij,pq->spiqj', w, eye).reshape(nslab, MXU_DIM, MXU_DIM)

    return jnp.concatenate([slab_diag(wa), slab_diag(wx)], axis=2).astype(BF16)


def _rnn_fwd(xr, conv_w, conv_b, wa, ba, wx, bx, lam):
    b, s, d = xr.shape
    tm = min(RNN_TM, s)
    nt = s // tm
    hb = tm // SUBLANES
    wcat = _gate_slabs(wa, wx, d)
    main = pl.BlockSpec((1, tm, d), lambda bi, t: (bi, t, 0))
    const = lambda shape: pl.BlockSpec(shape, lambda bi, t: (0,) * len(shape))
    return pl.pallas_call(
        _rnn_fwd_kernel,
        grid=(b, nt),
        in_specs=[
            main,
            pl.BlockSpec((1, SUBLANES, d), lambda bi, t: (bi, jnp.maximum(t * hb - 1, 0), 0)),
            pl.BlockSpec((1, SUBLANES, d), lambda bi, t: (bi, jnp.minimum((t + 1) * hb, s // SUBLANES - 1), 0)),
            const((CONV_W, d)), const((1, d)), const(wcat.shape), const((1, d)), const((1, d)), const((1, d)),
        ],
        out_specs=[main, main],
        out_shape=[jax.ShapeDtypeStruct((b, s, d), F32), jax.ShapeDtypeStruct((b, s, d), F32)],
        scratch_shapes=[
            pltpu.VMEM((tm + 2 * SUBLANES, d), F32),
            pltpu.VMEM((tm, d), F32),
            pltpu.VMEM((tm, d), F32),
            pltpu.VMEM((1, d), F32),
        ],
        compiler_params=_params("parallel", "arbitrary"),
        name="rnn_fwd",
    )(xr, xr, xr, conv_w, conv_b.reshape(1, d), wcat, ba.reshape(1, d), bx.reshape(1, d), lam.reshape(1, d))


def _rnn_bwd(xc, wa, ba, wx, bx, lam, hf, gy):
    b, s, d = xc.shape
    tm = min(RNN_TM, s)
    nt = s // tm
    wcat = _gate_slabs(wa, wx, d)
    main = pl.BlockSpec((1, tm, d), lambda bi, t: (bi, nt - 1 - t, 0))
    const = lambda shape: pl.BlockSpec(shape, lambda bi, t: (0,) * len(shape))
    return pl.pallas_call(
        _rnn_bwd_kernel,
        grid=(b, nt),
        in_specs=[main, const(wcat.shape), const((1, d)), const((1, d)), const((1, d)), main, main],
        out_specs=main,
        out_shape=jax.ShapeDtypeStruct((b, s, d), BF16),
        scratch_shapes=[
            pltpu.VMEM((tm, d), F32),
            pltpu.VMEM((tm, d), F32),
            pltpu.VMEM((tm, d), F32),
            pltpu.VMEM((1, d), F32),
        ],
        compiler_params=_params("parallel", "arbitrary"),
        name="rnn_bwd",
    )(xc, wcat, ba.reshape(1, d), bx.reshape(1, d), lam.reshape(1, d), hf, gy)


def _merge_kernel(x_ref, at_ref, rg_ref, gate_ref, wa_ref, wr_ref, wo_ref, o_ref):
    d = x_ref.shape[1]
    n_tiles, n_kv, hd, gtq = at_ref.shape
    tq = gtq // GROUP
    slab = GROUP * hd
    a_tiles = []
    for t in range(n_tiles):
        a = None
        for j in range(n_kv):
            o_t = at_ref[t, j].astype(F32)
            lhs = jnp.concatenate([o_t[:, g * tq:(g + 1) * tq] for g in range(GROUP)], axis=0).T.astype(BF16)
            term = jnp.dot(lhs, wa_ref[j * slab:(j + 1) * slab, :], preferred_element_type=F32)
            a = term if a is None else a + term
        a_tiles.append(a)
    a = jnp.concatenate(a_tiles, axis=0)
    r = jnp.dot(rg_ref[...], wr_ref[...], preferred_element_type=F32)
    merged = gate_ref[:, :d].astype(F32) * a + gate_ref[:, d:].astype(F32) * r
    o_ref[...] = x_ref[...] + jnp.dot(merged.astype(BF16), wo_ref[...], preferred_element_type=F32)


def _merge(x2, attn_t, rg2, gate2, w_attn_o, w_rnn_o, w_out, layer):
    t, d = x2.shape
    tq = attn_t.shape[4] // GROUP
    tm = min(OUT_TM, t)
    n_tiles = tm // tq
    at = attn_t.reshape((-1,) + attn_t.shape[2:])
    rows = lambda width: pl.BlockSpec((tm, width), lambda i: (i, 0))
    const = lambda shape: pl.BlockSpec((None,) + shape[1:], lambda i: (layer, 0, 0))
    return pl.pallas_call(
        _merge_kernel,
        grid=(t // tm,),
        in_specs=[rows(d), pl.BlockSpec((n_tiles,) + at.shape[1:], lambda i: (i, 0, 0, 0)),
                  rows(rg2.shape[1]), rows(2 * d),
                  const(w_attn_o.shape), const(w_rnn_o.shape), const(w_out.shape)],
        out_specs=rows(d),
        out_shape=jax.ShapeDtypeStruct((t, d), F32),
        compiler_params=_params("parallel"),
        name="merge",
    )(x2, at, rg2, gate2, w_attn_o, w_rnn_o, w_out)


def kernel(x, ffn1_norm, ffn1_w1, ffn1_w2, mix_norm, w_in, b_gate, q_norm, k_norm, w_attn_o, conv_w, conv_b,
           lru_wa, lru_ba, lru_wx, lru_bx, lru_lambda, w_rnn_o, w_out, ffn2_norm, ffn2_w1, ffn2_w2):
    b, s, d = x.shape
    depth = w_in.shape[0]
    x2 = x.reshape(b * s, d)
    cast = lambda w: w.astype(BF16)
    f1w1, f1w2, f2w1, f2w2 = cast(ffn1_w1), cast(ffn1_w2), cast(ffn2_w1), cast(ffn2_w2)
    w_in_b, w_ao, w_ro, w_o = cast(w_in), cast(w_attn_o), cast(w_rnn_o), cast(w_out)
    for l in range(depth):
        x2 = _ffn(x2, ffn1_norm[l], f1w1, f1w2, l)
        qt, k, vt, xr, gy, gates = _in_proj(x2.reshape(b, s, d), mix_norm[l], w_in_b, l, b_gate[l],
                                           q_norm[l], k_norm[l])
        attn = _attention(qt, k, vt)
        hf, xc = _rnn_fwd(xr, conv_w[l], conv_b[l], lru_wa[l, 0], lru_ba[l, 0], lru_wx[l, 0], lru_bx[l, 0],
                          lru_lambda[l, 0])
        rg = _rnn_bwd(xc, lru_wa[l, 1], lru_ba[l, 1], lru_wx[l, 1], lru_bx[l, 1], lru_lambda[l, 1], hf, gy)
        x2 = _merge(x2, attn, rg.reshape(b * s, -1), gates.reshape(b * s, -1), w_ao, w_ro, w_o, l)
        x2 = _ffn(x2, ffn2_norm[l], f2w1, f2w2, l)
    return x2.reshape(b, s, d)
```

```python
import functools
import math

import jax
import jax.numpy as jnp
from jax import lax
from jax.experimental import pallas as pl
from jax.experimental.pallas import tpu as pltpu

F32 = jnp.float32
BF16 = jnp.bfloat16

N_HEADS = 16
N_KV_HEADS = 4
HEAD_DIM = 64
GROUP = N_HEADS // N_KV_HEADS
VT_ROWS = HEAD_DIM + 16
SAFE_SHIFT = 60.0
ROPE_THETA = 10000.0
ROPE_QUARTER = HEAD_DIM // 4
GRID_W = 64
CONV_W = 4
CONV_LEFT = 2
LRU_C = 8.0
FFN_RESID = 0.5
EPS = 1e-6
LOG2E = math.log2(math.e)

LANES = 128
SUBLANES = 8
MXU_DIM = 256
VMEM_LIMIT_BYTES = 56 * 1024 * 1024

FFN_TM = 512
FFN_TF = 256
PROJ_TM = 512
PROJ_ROW_SPLIT = 2
ATT_TK = 1024
ATT_UNROLL = 2
RNN_TM = 512
RNN_ROW_BLOCK = 256
OUT_TM = 1024


def _params(*sem):
    return pltpu.CompilerParams(dimension_semantics=sem, vmem_limit_bytes=VMEM_LIMIT_BYTES)


def _rms_scale(x):
    return lax.rsqrt(jnp.sum(x * x, axis=-1, keepdims=True) * (1.0 / x.shape[-1]) + EPS)


def _sigmoid(x):
    return 1.0 / (1.0 + jnp.exp2(x * -LOG2E))


def _ffn_kernel(x_ref, g_ref, w1_ref, w2_ref, o_ref):
    d_ff = w2_ref.shape[0]
    x = x_ref[...]
    h = (x * _rms_scale(x) * g_ref[...]).astype(BF16)
    acc = None
    for c in range(0, d_ff, FFN_TF):
        g = jnp.dot(h, w1_ref[:, c:c + FFN_TF], preferred_element_type=F32)
        u = jnp.dot(h, w1_ref[:, d_ff + c:d_ff + c + FFN_TF], preferred_element_type=F32)
        act = (g * _sigmoid(g) * u).astype(BF16)
        part = jnp.dot(act, w2_ref[c:c + FFN_TF, :], preferred_element_type=F32)
        acc = part if acc is None else acc + part
    o_ref[...] = x + FFN_RESID * acc


def _ffn(x2, gain, w1, w2, layer):
    t, d = x2.shape
    tm = min(FFN_TM, t)
    assert t % tm == 0 and w2.shape[1] % FFN_TF == 0 and w1.shape[2] == 2 * w2.shape[1]
    resident = lambda shape: pl.BlockSpec((None,) + shape[1:], lambda i: (layer, 0, 0),
                                          pipeline_mode=pl.Buffered(1))
    return pl.pallas_call(
        _ffn_kernel,
        grid=(t // tm,),
        in_specs=[
            pl.BlockSpec((tm, d), lambda i: (i, 0)),
            pl.BlockSpec((1, d), lambda i: (0, 0)),
            resident(w1.shape),
            resident(w2.shape),
        ],
        out_specs=pl.BlockSpec((tm, d), lambda i: (i, 0)),
        out_shape=jax.ShapeDtypeStruct((t, d), F32),
        compiler_params=_params("parallel"),
        name="ffn",
    )(x2, gain.reshape(1, d), w1, w2)


def _rope_tables(seq):
    pos = jnp.arange(seq, dtype=jnp.int32)
    row = (pos // GRID_W).astype(F32)
    col = (pos % GRID_W).astype(F32)
    half = HEAD_DIM // 2
    inv = ROPE_THETA ** (-jnp.arange(0, half, 2, dtype=F32) / half)
    ang_r = row[:, None] * inv[None, :]
    ang_c = col[:, None] * inv[None, :]
    zero = jnp.zeros_like(ang_r)
    cos = jnp.concatenate([jnp.cos(ang_r), jnp.cos(ang_r), jnp.cos(ang_c), jnp.cos(ang_c)], axis=1)
    s_up = jnp.concatenate([-jnp.sin(ang_r), zero, -jnp.sin(ang_c), zero], axis=1)
    s_dn = jnp.concatenate([zero, jnp.sin(ang_r), zero, jnp.sin(ang_c)], axis=1)
    rep = LANES // HEAD_DIM
    return tuple(jnp.tile(a, (1, rep)) for a in (cos, s_up, s_dn))


def _head_norm_rope(z, seg, gain, cos, s_up, s_dn):
    width = z.shape[1]
    sq = (z * z).astype(BF16)
    ms = jnp.concatenate(
        [jnp.dot(sq[:, c:c + MXU_DIM], seg, preferred_element_type=F32) for c in range(0, width, MXU_DIM)],
        axis=1)
    zn = z * lax.rsqrt(ms + EPS) * gain
    rep = width // LANES
    widen = lambda a: jnp.concatenate([a] * rep, axis=1)
    up = pltpu.roll(zn, width - ROPE_QUARTER, axis=1)
    dn = pltpu.roll(zn, ROPE_QUARTER, axis=1)
    return zn * widen(cos) + up * widen(s_up) + dn * widen(s_dn)


def _proj_kernel(x_ref, gn_ref, w_ref, bg_ref, gq_ref, gk_ref, seg_ref, cos_ref, sup_ref, sdn_ref,
                 qt_ref, k_ref, vt_ref, xr_ref, gy_ref, gate_ref):
    tm = x_ref.shape[1]
    rb = tm // PROJ_ROW_SPLIT
    widths = (N_HEADS * HEAD_DIM, k_ref.shape[1] * HEAD_DIM, k_ref.shape[1] * HEAD_DIM,
              xr_ref.shape[2], gy_ref.shape[2], gate_ref.shape[2])
    starts = [sum(widths[:i]) for i in range(len(widths))]
    wq_ref, wk_ref, wv_ref, wx_ref, wy_ref, wg_ref = (w_ref.at[:, o:o + n] for o, n in zip(starts, widths))
    seg = seg_ref[...]
    qscale = HEAD_DIM ** -0.5 * LOG2E
    for blk in range(PROJ_ROW_SPLIT):
        rows = slice(blk * rb, (blk + 1) * rb)
        x = x_ref[0, rows, :]
        h = (x * _rms_scale(x) * gn_ref[...]).astype(BF16)
        cos, s_up, s_dn = cos_ref[rows, :], sup_ref[rows, :], sdn_ref[rows, :]

        q = jnp.dot(h, wq_ref[...], preferred_element_type=F32)
        q_t = (_head_norm_rope(q, seg, gq_ref[...], cos, s_up, s_dn) * qscale).T.astype(BF16)
        for j in range(N_KV_HEADS):
            for g in range(GROUP):
                r0 = (j * GROUP + g) * HEAD_DIM
                qt_ref[0, 0, j, :, g * tm + blk * rb:g * tm + (blk + 1) * rb] = q_t[r0:r0 + HEAD_DIM, :]

        k = jnp.dot(h, wk_ref[...], preferred_element_type=F32)
        k = _head_norm_rope(k, seg, gk_ref[...], cos, s_up, s_dn).astype(BF16)
        vt = jnp.dot(h, wv_ref[...], preferred_element_type=F32).T.astype(BF16)
        for j in range(N_KV_HEADS):
            k_ref[0, j, rows, :] = k[:, j * HEAD_DIM:(j + 1) * HEAD_DIM]
            vt_ref[0, j, 0, 0:HEAD_DIM, rows] = vt[j * HEAD_DIM:(j + 1) * HEAD_DIM, :]
            vt_ref[0, j, 0, HEAD_DIM:VT_ROWS, rows] = jnp.ones((VT_ROWS - HEAD_DIM, rb), BF16)

        xr_ref[0, rows, :] = jnp.dot(h, wx_ref[...], preferred_element_type=F32)

        y = jnp.dot(h, wy_ref[...], preferred_element_type=F32)
        gelu = 0.5 * y * (1.0 + jnp.tanh(math.sqrt(2.0 / math.pi) * (y + 0.044715 * (y * y * y))))
        gy_ref[0, rows, :] = gelu.astype(BF16)

        gl = jnp.dot(h, wg_ref[...], preferred_element_type=F32) + bg_ref[...]
        gate_ref[0, rows, :] = _sigmoid(gl).astype(BF16)


def _in_proj(x, gain, w_in, layer, b_gate, q_gain, k_gain):
    b, s, d = x.shape
    q_cols = N_HEADS * HEAD_DIM
    kv_cols = N_KV_HEADS * HEAD_DIM
    d_rnn = (w_in.shape[2] - q_cols - 2 * kv_cols - 2 * d) // 2
    tm = min(PROJ_TM, s)
    nt = s // tm
    assert s % tm == 0 and tm % (PROJ_ROW_SPLIT * LANES) == 0 and s % GRID_W == 0
    cos, s_up, s_dn = _rope_tables(s)
    lane = jnp.arange(MXU_DIM)
    seg = jnp.where(lane[:, None] // HEAD_DIM == lane[None, :] // HEAD_DIM, 1.0 / HEAD_DIM, 0.0).astype(BF16)
    const = lambda shape: pl.BlockSpec(shape, lambda bi, ti: (0,) * len(shape))
    tab = pl.BlockSpec((tm, LANES), lambda bi, ti: (ti, 0))
    row = lambda width: pl.BlockSpec((1, tm, width), lambda bi, ti: (bi, ti, 0))
    head = pl.BlockSpec((1, N_KV_HEADS, tm, HEAD_DIM), lambda bi, ti: (bi, 0, ti, 0))
    head_t = pl.BlockSpec((1, N_KV_HEADS, 1, VT_ROWS, tm), lambda bi, ti: (bi, 0, ti, 0, 0))
    return pl.pallas_call(
        _proj_kernel,
        grid=(b, nt),
        in_specs=[row(d), const((1, d)),
                  pl.BlockSpec((None,) + w_in.shape[1:], lambda bi, ti: (layer, 0, 0), pipeline_mode=pl.Buffered(1)),
                  const((1, 2 * d)), const((1, q_cols)), const((1, kv_cols)), const((MXU_DIM, MXU_DIM)), tab, tab, tab],
        out_specs=[pl.BlockSpec((1, 1, N_KV_HEADS, HEAD_DIM, GROUP * tm), lambda bi, ti: (bi, ti, 0, 0, 0)),
                   head, head_t, row(d_rnn), row(d_rnn), row(2 * d)],
        out_shape=[
            jax.ShapeDtypeStruct((b, nt, N_KV_HEADS, HEAD_DIM, GROUP * tm), BF16),
            jax.ShapeDtypeStruct((b, N_KV_HEADS, s, HEAD_DIM), BF16),
            jax.ShapeDtypeStruct((b, N_KV_HEADS, nt, VT_ROWS, tm), BF16),
            jax.ShapeDtypeStruct((b, s, d_rnn), F32),
            jax.ShapeDtypeStruct((b, s, d_rnn), BF16),
            jax.ShapeDtypeStruct((b, s, 2 * d), BF16),
        ],
        compiler_params=_params("parallel", "parallel"),
        name="in_proj",
    )(x, gain.reshape(1, d), w_in, b_gate.reshape(1, 2 * d),
      jnp.tile(q_gain, N_HEADS).reshape(1, q_cols), jnp.tile(k_gain, N_KV_HEADS).reshape(1, kv_cols),
      seg, cos, s_up, s_dn)


def _attn_kernel(q_ref, k_ref, vt_ref, o_ref, shift_ref, m_ref, acc_ref, kmax_ref, *, tk):
    qt = q_ref.at[0, 0, 0]
    seq = k_ref.shape[2]
    kc = vt_ref.shape[4]
    nsub = tk // kc
    nchunk = seq // tk

    @pl.when(pl.program_id(2) == 0)
    def _():
        def norm_chunk(c, best):
            kk = k_ref[0, 0, pl.ds(pl.multiple_of(c * tk, tk), tk), :].astype(F32)
            return jnp.maximum(best, jnp.max(jnp.sum(kk * kk, axis=1, keepdims=True), axis=0, keepdims=True))
        kmax_ref[...] = jnp.sqrt(lax.fori_loop(0, nchunk, norm_chunk, jnp.zeros((1, 1), F32)))

    qf = qt[...].astype(F32)
    shift_ref[...] = jnp.sqrt(jnp.sum(qf * qf, axis=0, keepdims=True)) * kmax_ref[...]
    acc_ref[...] = jnp.zeros_like(acc_ref)
    safe = jnp.max(shift_ref[...]) <= SAFE_SHIFT

    def pv(c, p_t):
        vt = jnp.concatenate([vt_ref[0, 0, c * nsub + sub] for sub in range(nsub)], axis=1)
        return jnp.dot(vt, p_t, preferred_element_type=F32)

    def scores(c):
        k = k_ref[0, 0, pl.ds(pl.multiple_of(c * tk, tk), tk), :]
        return jnp.dot(k, qt[...], preferred_element_type=F32)

    @pl.when(safe)
    def _():
        def body(c, carry):
            p_t = jnp.exp2(scores(c) - shift_ref[...]).astype(BF16)
            acc_ref[...] += pv(c, p_t)
            return carry
        lax.fori_loop(0, nchunk, body, 0, unroll=min(ATT_UNROLL, nchunk))

    @pl.when(jnp.logical_not(safe))
    def _():
        m_ref[...] = jnp.full_like(m_ref, -jnp.inf)

        def body(c, carry):
            s_t = scores(c)
            m_prev = m_ref[...]
            m_new = jnp.maximum(m_prev, jnp.max(s_t, axis=0, keepdims=True))
            p_t = jnp.exp2(s_t - m_new).astype(BF16)
            acc_ref[...] = jnp.exp2(m_prev - m_new) * acc_ref[...] + pv(c, p_t)
            m_ref[...] = m_new
            return carry
        lax.fori_loop(0, nchunk, body, 0)

    o_ref[0, 0, 0] = (acc_ref[0:HEAD_DIM, :] / acc_ref[HEAD_DIM:HEAD_DIM + 1, :]).astype(o_ref.dtype)


def _attention(qt, k, vt):
    b, nq, _, _, gtq = qt.shape
    s = k.shape[2]
    tk = min(ATT_TK, s)
    nchunk, vt_rows, kc = vt.shape[2:]
    assert s % tk == 0 and tk % kc == 0 and (s // tk) % min(ATT_UNROLL, s // tk) == 0
    tile = pl.BlockSpec((1, 1, 1, HEAD_DIM, gtq), lambda bi, j, i: (bi, i, j, 0, 0))
    return pl.pallas_call(
        functools.partial(_attn_kernel, tk=tk),
        grid=(b, N_KV_HEADS, nq),
        in_specs=[
            tile,
            pl.BlockSpec((1, 1, s, HEAD_DIM), lambda bi, j, i: (bi, j, 0, 0)),
            pl.BlockSpec((1, 1, nchunk, vt_rows, kc), lambda bi, j, i: (bi, j, 0, 0, 0)),
        ],
        out_specs=tile,
        out_shape=jax.ShapeDtypeStruct(qt.shape, BF16),
        scratch_shapes=[
            pltpu.VMEM((1, gtq), F32),
            pltpu.VMEM((1, gtq), F32),
            pltpu.VMEM((vt_rows, gtq), F32),
            pltpu.VMEM((1, 1), F32),
        ],
        compiler_params=_params("parallel", "arbitrary", "arbitrary"),
        name="attention",
    )(qt, k, vt)


def _lru_gates_and_scan(xc, wcat_ref, ba_ref, bx_ref, lam_ref, a_ref, u_ref, hs_ref, hc_ref, reverse):
    tm, d = xc.shape
    rb = min(RNN_ROW_BLOCK, tm)
    lam = lam_ref[...]
    softplus_neg_lam = jnp.maximum(-lam, 0.0) + jnp.log1p(jnp.exp(-jnp.abs(lam)))
    decay_rate = (-LRU_C * LOG2E) * softplus_neg_lam
    h = hc_ref[...]
    order = range(tm // rb - 1, -1, -1) if reverse else range(tm // rb)
    for blk in order:
        rows = slice(blk * rb, (blk + 1) * rb)
        xb = xc[rows, :]
        xbb = xb.astype(BF16)
        for c in range(d // MXU_DIM):
            sl = slice(c * MXU_DIM, (c + 1) * MXU_DIM)
            g = jnp.dot(xbb[:, sl], wcat_ref[c], preferred_element_type=F32)
            r = _sigmoid(g[:, :MXU_DIM] + ba_ref[:, sl])
            i = _sigmoid(g[:, MXU_DIM:] + bx_ref[:, sl])
            a = jnp.exp2(r * decay_rate[:, sl])
            a_ref[rows, sl] = a
            y = 1.0 - a * a
            u_ref[rows, sl] = y * lax.rsqrt(jnp.maximum(y, 1e-30)) * (i * xb[:, sl])
        for row in (range((blk + 1) * rb - 1, blk * rb - 1, -1) if reverse else range(blk * rb, (blk + 1) * rb)):
            h = a_ref[row:row + 1, :] * h + u_ref[row:row + 1, :]
            hs_ref[row:row + 1, :] = h
    hc_ref[...] = h


def _rnn_fwd_kernel(xr_ref, prev_ref, next_ref, cw_ref, cb_ref, wcat_ref, ba_ref, bx_ref, lam_ref,
                    hf_ref, xc_ref, xpad_ref, a_ref, u_ref, hc_ref):
    t = pl.program_id(1)
    nt = pl.num_programs(1)
    tm = xr_ref.shape[1]
    halo = SUBLANES

    @pl.when(t == 0)
    def _():
        hc_ref[...] = jnp.zeros_like(hc_ref)

    xpad_ref[halo:halo + tm, :] = xr_ref[0]
    xpad_ref[0:halo, :] = jnp.where(t > 0, prev_ref[0], 0.0)
    xpad_ref[halo + tm:halo + tm + halo, :] = jnp.where(t < nt - 1, next_ref[0], 0.0)
    xc = cb_ref[...] + sum(
        cw_ref[kk:kk + 1, :] * xpad_ref[halo - CONV_LEFT + kk:halo - CONV_LEFT + kk + tm, :]
        for kk in range(CONV_W))
    xc_ref[0] = xc
    _lru_gates_and_scan(xc, wcat_ref, ba_ref, bx_ref, lam_ref, a_ref, u_ref, hf_ref.at[0], hc_ref, reverse=False)


def _rnn_bwd_kernel(xc_ref, wcat_ref, ba_ref, bx_ref, lam_ref, hf_ref, gy_ref, o_ref, a_ref, u_ref, hs_ref, hc_ref):
    @pl.when(pl.program_id(1) == 0)
    def _():
        hc_ref[...] = jnp.zeros_like(hc_ref)

    _lru_gates_and_scan(xc_ref[0], wcat_ref, ba_ref, bx_ref, lam_ref, a_ref, u_ref, hs_ref, hc_ref, reverse=True)
    o_ref[0] = ((hf_ref[0] + hs_ref[...]) * gy_ref[0].astype(F32)).astype(o_ref.dtype)


def _gate_slabs(wa, wx, d):
    nb, blk, _ = wa.shape
    per = MXU_DIM // blk
    nslab = d // MXU_DIM

    def slab_diag(w):
        w = w.reshape(nslab, per, blk, blk)
        eye = jnp.eye(per, dtype=w.dtype)
        return jnp.einsum('spij,pq->spiqj', w, eye).reshape(nslab, MXU_DIM, MXU_DIM)

    return jnp.concatenate([slab_diag(wa), slab_diag(wx)], axis=2).astype(BF16)


def _rnn_fwd(xr, conv_w, conv_b, wa, ba, wx, bx, lam):
    b, s, d = xr.shape
    tm = min(RNN_TM, s)
    assert s % tm == 0 and tm % min(RNN_ROW_BLOCK, tm) == 0
    nt = s // tm
    hb = tm // SUBLANES
    wcat = _gate_slabs(wa, wx, d)
    main = pl.BlockSpec((1, tm, d), lambda bi, t: (bi, t, 0))
    const = lambda shape: pl.BlockSpec(shape, lambda bi, t: (0,) * len(shape))
    return pl.pallas_call(
        _rnn_fwd_kernel,
        grid=(b, nt),
        in_specs=[
            main,
            pl.BlockSpec((1, SUBLANES, d), lambda bi, t: (bi, jnp.maximum(t * hb - 1, 0), 0)),
            pl.BlockSpec((1, SUBLANES, d), lambda bi, t: (bi, jnp.minimum((t + 1) * hb, s // SUBLANES - 1), 0)),
            const((CONV_W, d)), const((1, d)), const(wcat.shape), const((1, d)), const((1, d)), const((1, d)),
        ],
        out_specs=[main, main],
        out_shape=[jax.ShapeDtypeStruct((b, s, d), F32), jax.ShapeDtypeStruct((b, s, d), F32)],
        scratch_shapes=[
            pltpu.VMEM((tm + 2 * SUBLANES, d), F32),
            pltpu.VMEM((tm, d), F32),
            pltpu.VMEM((tm, d), F32),
            pltpu.VMEM((1, d), F32),
        ],
        compiler_params=_params("parallel", "arbitrary"),
        name="rnn_fwd",
    )(xr, xr, xr, conv_w, conv_b.reshape(1, d), wcat, ba.reshape(1, d), bx.reshape(1, d), lam.reshape(1, d))


def _rnn_bwd(xc, wa, ba, wx, bx, lam, hf, gy):
    b, s, d = xc.shape
    tm = min(RNN_TM, s)
    assert s % tm == 0 and tm % min(RNN_ROW_BLOCK, tm) == 0
    nt = s // tm
    wcat = _gate_slabs(wa, wx, d)
    main = pl.BlockSpec((1, tm, d), lambda bi, t: (bi, nt - 1 - t, 0))
    const = lambda shape: pl.BlockSpec(shape, lambda bi, t: (0,) * len(shape))
    return pl.pallas_call(
        _rnn_bwd_kernel,
        grid=(b, nt),
        in_specs=[main, const(wcat.shape), const((1, d)), const((1, d)), const((1, d)), main, main],
        out_specs=main,
        out_shape=jax.ShapeDtypeStruct((b, s, d), BF16),
        scratch_shapes=[
            pltpu.VMEM((tm, d), F32),
            pltpu.VMEM((tm, d), F32),
            pltpu.VMEM((tm, d), F32),
            pltpu.VMEM((1, d), F32),
        ],
        compiler_params=_params("parallel", "arbitrary"),
        name="rnn_bwd",
    )(xc, wcat, ba.reshape(1, d), bx.reshape(1, d), lam.reshape(1, d), hf, gy)


def _merge_kernel(x_ref, at_ref, rg_ref, gate_ref, wa_ref, wr_ref, wo_ref, o_ref):
    d = x_ref.shape[1]
    n_tiles, n_kv, hd, gtq = at_ref.shape
    tq = gtq // GROUP
    slab = GROUP * hd
    a_tiles = []
    for t in range(n_tiles):
        a = None
        for j in range(n_kv):
            o_t = at_ref[t, j].astype(F32)
            lhs = jnp.concatenate([o_t[:, g * tq:(g + 1) * tq] for g in range(GROUP)], axis=0).T.astype(BF16)
            term = jnp.dot(lhs, wa_ref[j * slab:(j + 1) * slab, :], preferred_element_type=F32)
            a = term if a is None else a + term
        a_tiles.append(a)
    a = jnp.concatenate(a_tiles, axis=0)
    r = jnp.dot(rg_ref[...], wr_ref[...], preferred_element_type=F32)
    merged = gate_ref[:, :d].astype(F32) * a + gate_ref[:, d:].astype(F32) * r
    o_ref[...] = x_ref[...] + jnp.dot(merged.astype(BF16), wo_ref[...], preferred_element_type=F32)


def _merge(x2, attn_t, rg2, gate2, w_attn_o, w_rnn_o, w_out, layer):
    t, d = x2.shape
    tq = attn_t.shape[4] // GROUP
    tm = min(OUT_TM, t)
    assert t % tm == 0 and tm % tq == 0
    n_tiles = tm // tq
    at = attn_t.reshape((-1,) + attn_t.shape[2:])
    rows = lambda width: pl.BlockSpec((tm, width), lambda i: (i, 0))
    const = lambda shape: pl.BlockSpec((None,) + shape[1:], lambda i: (layer, 0, 0))
    return pl.pallas_call(
        _merge_kernel,
        grid=(t // tm,),
        in_specs=[rows(d), pl.BlockSpec((n_tiles,) + at.shape[1:], lambda i: (i, 0, 0, 0)),
                  rows(rg2.shape[1]), rows(2 * d),
                  const(w_attn_o.shape), const(w_rnn_o.shape), const(w_out.shape)],
        out_specs=rows(d),
        out_shape=jax.ShapeDtypeStruct((t, d), F32),
        compiler_params=_params("parallel"),
        name="merge",
    )(x2, at, rg2, gate2, w_attn_o, w_rnn_o, w_out)


def kernel(x, ffn1_norm, ffn1_w1, ffn1_w2, mix_norm, w_in, b_gate, q_norm, k_norm, w_attn_o, conv_w, conv_b,
           lru_wa, lru_ba, lru_wx, lru_bx, lru_lambda, w_rnn_o, w_out, ffn2_norm, ffn2_w1, ffn2_w2):
    b, s, d = x.shape
    depth = w_in.shape[0]
    x2 = x.reshape(b * s, d)
    cast = lambda w: w.astype(BF16)
    f1w1, f1w2, f2w1, f2w2 = cast(ffn1_w1), cast(ffn1_w2), cast(ffn2_w1), cast(ffn2_w2)
    w_in_b, w_ao, w_ro, w_o = cast(w_in), cast(w_attn_o), cast(w_rnn_o), cast(w_out)
    for l in range(depth):
        x2 = _ffn(x2, ffn1_norm[l], f1w1, f1w2, l)
        qt, k, vt, xr, gy, gates = _in_proj(x2.reshape(b, s, d), mix_norm[l], w_in_b, l, b_gate[l],
                                           q_norm[l], k_norm[l])
        attn = _attention(qt, k, vt)
        hf, xc = _rnn_fwd(xr, conv_w[l], conv_b[l], lru_wa[l, 0], lru_ba[l, 0], lru_wx[l, 0], lru_bx[l, 0],
                          lru_lambda[l, 0])
        rg = _rnn_bwd(xc, lru_wa[l, 1], lru_ba[l, 1], lru_wx[l, 1], lru_bx[l, 1], lru_lambda[l, 1], hf, gy)
        x2 = _merge(x2, attn, rg.reshape(b * s, -1), gates.reshape(b * s, -1), w_ao, w_ro, w_o, l)
        x2 = _ffn(x2, ffn2_norm[l], f2w1, f2w2, l)
    return x2.reshape(b, s, d)
```

```python
import functools
import math

import jax
import jax.numpy as jnp
from jax import lax
from jax.experimental import pallas as pl
from jax.experimental.pallas import tpu as pltpu

F32 = jnp.float32
BF16 = jnp.bfloat16

N_HEADS = 16
N_KV_HEADS = 4
HEAD_DIM = 64
GROUP = N_HEADS // N_KV_HEADS
VT_ROWS = HEAD_DIM + 16
SAFE_SHIFT = 60.0
ROPE_THETA = 10000.0
ROPE_QUARTER = HEAD_DIM // 4
GRID_W = 64
CONV_W = 4
CONV_LEFT = 2
LRU_C = 8.0
FFN_RESID = 0.5
EPS = 1e-6
LOG2E = math.log2(math.e)

LANES = 128
SUBLANES = 8
MXU_DIM = 256
VMEM_LIMIT_BYTES = 56 * 1024 * 1024

FFN_TM = 512
FFN_TF = 256
PROJ_TM = 512
PROJ_ROW_SPLIT = 2
ATT_TK = 1024
ATT_UNROLL = 4
RNN_TM = 512
RNN_ROW_BLOCK = 256
OUT_TM = 1024


def _params(*sem):
    return pltpu.CompilerParams(dimension_semantics=sem, vmem_limit_bytes=VMEM_LIMIT_BYTES)


def _rms_scale(x):
    return lax.rsqrt(jnp.sum(x * x, axis=-1, keepdims=True) * (1.0 / x.shape[-1]) + EPS)


def _sigmoid(x):
    return 1.0 / (1.0 + jnp.exp2(x * -LOG2E))


def _ffn_kernel(x_ref, g_ref, w1_ref, w2_ref, o_ref):
    d_ff = w2_ref.shape[0]
    x = x_ref[...]
    h = (x * _rms_scale(x) * g_ref[...]).astype(BF16)
    acc = None
    for c in range(0, d_ff, FFN_TF):
        g = jnp.dot(h, w1_ref[:, c:c + FFN_TF], preferred_element_type=F32)
        u = jnp.dot(h, w1_ref[:, d_ff + c:d_ff + c + FFN_TF], preferred_element_type=F32)
        act = (g * _sigmoid(g) * u).astype(BF16)
        part = jnp.dot(act, w2_ref[c:c + FFN_TF, :], preferred_element_type=F32)
        acc = part if acc is None else acc + part
    o_ref[...] = x + FFN_RESID * acc


def _ffn(x2, gain, w1, w2, layer):
    t, d = x2.shape
    tm = min(FFN_TM, t)
    assert t % tm == 0 and w2.shape[1] % FFN_TF == 0 and w1.shape[2] == 2 * w2.shape[1]
    resident = lambda shape: pl.BlockSpec((None,) + shape[1:], lambda i: (layer, 0, 0),
                                          pipeline_mode=pl.Buffered(1))
    return pl.pallas_call(
        _ffn_kernel,
        grid=(t // tm,),
        in_specs=[
            pl.BlockSpec((tm, d), lambda i: (i, 0)),
            pl.BlockSpec((1, d), lambda i: (0, 0)),
            resident(w1.shape),
            resident(w2.shape),
        ],
        out_specs=pl.BlockSpec((tm, d), lambda i: (i, 0)),
        out_shape=jax.ShapeDtypeStruct((t, d), F32),
        compiler_params=_params("parallel"),
        name="ffn",
    )(x2, gain.reshape(1, d), w1, w2)


def _rope_tables(seq):
    pos = jnp.arange(seq, dtype=jnp.int32)
    row = (pos // GRID_W).astype(F32)
    col = (pos % GRID_W).astype(F32)
    half = HEAD_DIM // 2
    inv = ROPE_THETA ** (-jnp.arange(0, half, 2, dtype=F32) / half)
    ang_r = row[:, None] * inv[None, :]
    ang_c = col[:, None] * inv[None, :]
    zero = jnp.zeros_like(ang_r)
    cos = jnp.concatenate([jnp.cos(ang_r), jnp.cos(ang_r), jnp.cos(ang_c), jnp.cos(ang_c)], axis=1)
    s_up = jnp.concatenate([-jnp.sin(ang_r), zero, -jnp.sin(ang_c), zero], axis=1)
    s_dn = jnp.concatenate([zero, jnp.sin(ang_r), zero, jnp.sin(ang_c)], axis=1)
    rep = LANES // HEAD_DIM
    return tuple(jnp.tile(a, (1, rep)) for a in (cos, s_up, s_dn))


def _head_norm_rope(z, seg, gain, cos, s_up, s_dn):
    width = z.shape[1]
    sq = (z * z).astype(BF16)
    ms = jnp.concatenate(
        [jnp.dot(sq[:, c:c + MXU_DIM], seg, preferred_element_type=F32) for c in range(0, width, MXU_DIM)],
        axis=1)
    zn = z * lax.rsqrt(ms + EPS) * gain
    rep = width // LANES
    widen = lambda a: jnp.concatenate([a] * rep, axis=1)
    up = pltpu.roll(zn, width - ROPE_QUARTER, axis=1)
    dn = pltpu.roll(zn, ROPE_QUARTER, axis=1)
    return zn * widen(cos) + up * widen(s_up) + dn * widen(s_dn)


def _proj_kernel(x_ref, gn_ref, w_ref, bg_ref, gq_ref, gk_ref, seg_ref, cos_ref, sup_ref, sdn_ref,
                 qt_ref, k_ref, vt_ref, xr_ref, gy_ref, gate_ref):
    tm = x_ref.shape[1]
    rb = tm // PROJ_ROW_SPLIT
    widths = (N_HEADS * HEAD_DIM, k_ref.shape[1] * HEAD_DIM, k_ref.shape[1] * HEAD_DIM,
              xr_ref.shape[2], gy_ref.shape[2], gate_ref.shape[2])
    starts = [sum(widths[:i]) for i in range(len(widths))]
    wq_ref, wk_ref, wv_ref, wx_ref, wy_ref, wg_ref = (w_ref.at[:, o:o + n] for o, n in zip(starts, widths))
    seg = seg_ref[...]
    qscale = HEAD_DIM ** -0.5 * LOG2E
    for blk in range(PROJ_ROW_SPLIT):
        rows = slice(blk * rb, (blk + 1) * rb)
        x = x_ref[0, rows, :]
        h = (x * _rms_scale(x) * gn_ref[...]).astype(BF16)
        cos, s_up, s_dn = cos_ref[rows, :], sup_ref[rows, :], sdn_ref[rows, :]

        q = jnp.dot(h, wq_ref[...], preferred_element_type=F32)
        q_t = (_head_norm_rope(q, seg, gq_ref[...], cos, s_up, s_dn) * qscale).T.astype(BF16)
        for j in range(N_KV_HEADS):
            for g in range(GROUP):
                r0 = (j * GROUP + g) * HEAD_DIM
                qt_ref[0, 0, j, :, g * tm + blk * rb:g * tm + (blk + 1) * rb] = q_t[r0:r0 + HEAD_DIM, :]

        k = jnp.dot(h, wk_ref[...], preferred_element_type=F32)
        k = _head_norm_rope(k, seg, gk_ref[...], cos, s_up, s_dn).astype(BF16)
        vt = jnp.dot(h, wv_ref[...], preferred_element_type=F32).T.astype(BF16)
        for j in range(N_KV_HEADS):
            k_ref[0, j, rows, :] = k[:, j * HEAD_DIM:(j + 1) * HEAD_DIM]
            vt_ref[0, j, 0, 0:HEAD_DIM, rows] = vt[j * HEAD_DIM:(j + 1) * HEAD_DIM, :]
            vt_ref[0, j, 0, HEAD_DIM:VT_ROWS, rows] = jnp.ones((VT_ROWS - HEAD_DIM, rb), BF16)

        xr_ref[0, rows, :] = jnp.dot(h, wx_ref[...], preferred_element_type=F32)

        y = jnp.dot(h, wy_ref[...], preferred_element_type=F32)
        gelu = 0.5 * y * (1.0 + jnp.tanh(math.sqrt(2.0 / math.pi) * (y + 0.044715 * (y * y * y))))
        gy_ref[0, rows, :] = gelu.astype(BF16)

        gl = jnp.dot(h, wg_ref[...], preferred_element_type=F32) + bg_ref[...]
        gate_ref[0, rows, :] = _sigmoid(gl).astype(BF16)


def _in_proj(x, gain, w_in, layer, b_gate, q_gain, k_gain):
    b, s, d = x.shape
    q_cols = N_HEADS * HEAD_DIM
    kv_cols = N_KV_HEADS * HEAD_DIM
    d_rnn = (w_in.shape[2] - q_cols - 2 * kv_cols - 2 * d) // 2
    tm = min(PROJ_TM, s)
    nt = s // tm
    assert s % tm == 0 and tm % (PROJ_ROW_SPLIT * LANES) == 0 and s % GRID_W == 0
    cos, s_up, s_dn = _rope_tables(s)
    lane = jnp.arange(MXU_DIM)
    seg = jnp.where(lane[:, None] // HEAD_DIM == lane[None, :] // HEAD_DIM, 1.0 / HEAD_DIM, 0.0).astype(BF16)
    const = lambda shape: pl.BlockSpec(shape, lambda bi, ti: (0,) * len(shape))
    tab = pl.BlockSpec((tm, LANES), lambda bi, ti: (ti, 0))
    row = lambda width: pl.BlockSpec((1, tm, width), lambda bi, ti: (bi, ti, 0))
    head = pl.BlockSpec((1, N_KV_HEADS, tm, HEAD_DIM), lambda bi, ti: (bi, 0, ti, 0))
    head_t = pl.BlockSpec((1, N_KV_HEADS, 1, VT_ROWS, tm), lambda bi, ti: (bi, 0, ti, 0, 0))
    return pl.pallas_call(
        _proj_kernel,
        grid=(b, nt),
        in_specs=[row(d), const((1, d)),
                  pl.BlockSpec((None,) + w_in.shape[1:], lambda bi, ti: (layer, 0, 0), pipeline_mode=pl.Buffered(1)),
                  const((1, 2 * d)), const((1, q_cols)), const((1, kv_cols)), const((MXU_DIM, MXU_DIM)), tab, tab, tab],
        out_specs=[pl.BlockSpec((1, 1, N_KV_HEADS, HEAD_DIM, GROUP * tm), lambda bi, ti: (bi, ti, 0, 0, 0)),
                   head, head_t, row(d_rnn), row(d_rnn), row(2 * d)],
        out_shape=[
            jax.ShapeDtypeStruct((b, nt, N_KV_HEADS, HEAD_DIM, GROUP * tm), BF16),
            jax.ShapeDtypeStruct((b, N_KV_HEADS, s, HEAD_DIM), BF16),
            jax.ShapeDtypeStruct((b, N_KV_HEADS, nt, VT_ROWS, tm), BF16),
            jax.ShapeDtypeStruct((b, s, d_rnn), F32),
            jax.ShapeDtypeStruct((b, s, d_rnn), BF16),
            jax.ShapeDtypeStruct((b, s, 2 * d), BF16),
        ],
        compiler_params=_params("parallel", "parallel"),
        name="in_proj",
    )(x, gain.reshape(1, d), w_in, b_gate.reshape(1, 2 * d),
      jnp.tile(q_gain, N_HEADS).reshape(1, q_cols), jnp.tile(k_gain, N_KV_HEADS).reshape(1, kv_cols),
      seg, cos, s_up, s_dn)


def _attn_kernel(q_ref, k_ref, vt_ref, o_ref, shift_ref, m_ref, acc_ref, kmax_ref, *, tk):
    qt = q_ref.at[0, 0, 0]
    seq = k_ref.shape[2]
    kc = vt_ref.shape[4]
    nsub = tk // kc
    nchunk = seq // tk

    @pl.when(pl.program_id(2) == 0)
    def _():
        def norm_chunk(c, best):
            kk = k_ref[0, 0, pl.ds(pl.multiple_of(c * tk, tk), tk), :].astype(F32)
            return jnp.maximum(best, jnp.max(jnp.sum(kk * kk, axis=1, keepdims=True), axis=0, keepdims=True))
        kmax_ref[...] = jnp.sqrt(lax.fori_loop(0, nchunk, norm_chunk, jnp.zeros((1, 1), F32)))

    qf = qt[...].astype(F32)
    shift_ref[...] = jnp.sqrt(jnp.sum(qf * qf, axis=0, keepdims=True)) * kmax_ref[...]
    acc_ref[...] = jnp.zeros_like(acc_ref)
    safe = jnp.max(shift_ref[...]) <= SAFE_SHIFT

    def pv(c, p_t):
        vt = jnp.concatenate([vt_ref[0, 0, c * nsub + sub] for sub in range(nsub)], axis=1)
        return jnp.dot(vt, p_t, preferred_element_type=F32)

    def scores(c):
        k = k_ref[0, 0, pl.ds(pl.multiple_of(c * tk, tk), tk), :]
        return jnp.dot(k, qt[...], preferred_element_type=F32)

    @pl.when(safe)
    def _():
        def body(c, carry):
            p_t = jnp.exp2(scores(c) - shift_ref[...]).astype(BF16)
            acc_ref[...] += pv(c, p_t)
            return carry
        lax.fori_loop(0, nchunk, body, 0, unroll=min(ATT_UNROLL, nchunk))

    @pl.when(jnp.logical_not(safe))
    def _():
        m_ref[...] = jnp.full_like(m_ref, -jnp.inf)

        def body(c, carry):
            s_t = scores(c)
            m_prev = m_ref[...]
            m_new = jnp.maximum(m_prev, jnp.max(s_t, axis=0, keepdims=True))
            p_t = jnp.exp2(s_t - m_new).astype(BF16)
            acc_ref[...] = jnp.exp2(m_prev - m_new) * acc_ref[...] + pv(c, p_t)
            m_ref[...] = m_new
            return carry
        lax.fori_loop(0, nchunk, body, 0)

    o_ref[0, 0, 0] = (acc_ref[0:HEAD_DIM, :] / acc_ref[HEAD_DIM:HEAD_DIM + 1, :]).astype(o_ref.dtype)


def _attention(qt, k, vt):
    b, nq, _, _, gtq = qt.shape
    s = k.shape[2]
    tk = min(ATT_TK, s)
    nchunk, vt_rows, kc = vt.shape[2:]
    assert s % tk == 0 and tk % kc == 0 and (s // tk) % min(ATT_UNROLL, s // tk) == 0
    tile = pl.BlockSpec((1, 1, 1, HEAD_DIM, gtq), lambda bi, j, i: (bi, i, j, 0, 0))
    return pl.pallas_call(
        functools.partial(_attn_kernel, tk=tk),
        grid=(b, N_KV_HEADS, nq),
        in_specs=[
            tile,
            pl.BlockSpec((1, 1, s, HEAD_DIM), lambda bi, j, i: (bi, j, 0, 0)),
            pl.BlockSpec((1, 1, nchunk, vt_rows, kc), lambda bi, j, i: (bi, j, 0, 0, 0)),
        ],
        out_specs=tile,
        out_shape=jax.ShapeDtypeStruct(qt.shape, BF16),
        scratch_shapes=[
            pltpu.VMEM((1, gtq), F32),
            pltpu.VMEM((1, gtq), F32),
            pltpu.VMEM((vt_rows, gtq), F32),
            pltpu.VMEM((1, 1), F32),
        ],
        compiler_params=_params("parallel", "arbitrary", "arbitrary"),
        name="attention",
    )(qt, k, vt)


def _lru_gates_and_scan(xc, wcat_ref, ba_ref, bx_ref, lam_ref, a_ref, u_ref, hs_ref, hc_ref, reverse):
    tm, d = xc.shape
    rb = min(RNN_ROW_BLOCK, tm)
    lam = lam_ref[...]
    softplus_neg_lam = jnp.maximum(-lam, 0.0) + jnp.log1p(jnp.exp(-jnp.abs(lam)))
    decay_rate = (-LRU_C * LOG2E) * softplus_neg_lam
    h = hc_ref[...]
    order = range(tm // rb - 1, -1, -1) if reverse else range(tm // rb)
    for blk in order:
        rows = slice(blk * rb, (blk + 1) * rb)
        xb = xc[rows, :]
        xbb = xb.astype(BF16)
        for c in range(d // MXU_DIM):
            sl = slice(c * MXU_DIM, (c + 1) * MXU_DIM)
            g = jnp.dot(xbb[:, sl], wcat_ref[c], preferred_element_type=F32)
            r = _sigmoid(g[:, :MXU_DIM] + ba_ref[:, sl])
            i = _sigmoid(g[:, MXU_DIM:] + bx_ref[:, sl])
            a = jnp.exp2(r * decay_rate[:, sl])
            a_ref[rows, sl] = a
            y = 1.0 - a * a
            u_ref[rows, sl] = y * lax.rsqrt(jnp.maximum(y, 1e-30)) * (i * xb[:, sl])
        for row in (range((blk + 1) * rb - 1, blk * rb - 1, -1) if reverse else range(blk * rb, (blk + 1) * rb)):
            h = a_ref[row:row + 1, :] * h + u_ref[row:row + 1, :]
            hs_ref[row:row + 1, :] = h
    hc_ref[...] = h


def _rnn_fwd_kernel(xr_ref, prev_ref, next_ref, cw_ref, cb_ref, wcat_ref, ba_ref, bx_ref, lam_ref,
                    hf_ref, xc_ref, xpad_ref, a_ref, u_ref, hc_ref):
    t = pl.program_id(1)
    nt = pl.num_programs(1)
    tm = xr_ref.shape[1]
    halo = SUBLANES

    @pl.when(t == 0)
    def _():
        hc_ref[...] = jnp.zeros_like(hc_ref)

    xpad_ref[halo:halo + tm, :] = xr_ref[0]
    xpad_ref[0:halo, :] = jnp.where(t > 0, prev_ref[0], 0.0)
    xpad_ref[halo + tm:halo + tm + halo, :] = jnp.where(t < nt - 1, next_ref[0], 0.0)
    xc = cb_ref[...] + sum(
        cw_ref[kk:kk + 1, :] * xpad_ref[halo - CONV_LEFT + kk:halo - CONV_LEFT + kk + tm, :]
        for kk in range(CONV_W))
    xc_ref[0] = xc
    _lru_gates_and_scan(xc, wcat_ref, ba_ref, bx_ref, lam_ref, a_ref, u_ref, hf_ref.at[0], hc_ref, reverse=False)


def _rnn_bwd_kernel(xc_ref, wcat_ref, ba_ref, bx_ref, lam_ref, hf_ref, gy_ref, o_ref, a_ref, u_ref, hs_ref, hc_ref):
    @pl.when(pl.program_id(1) == 0)
    def _():
        hc_ref[...] = jnp.zeros_like(hc_ref)

    _lru_gates_and_scan(xc_ref[0], wcat_ref, ba_ref, bx_ref, lam_ref, a_ref, u_ref, hs_ref, hc_ref, reverse=True)
    o_ref[0] = ((hf_ref[0] + hs_ref[...]) * gy_ref[0].astype(F32)).astype(o_ref.dtype)


def _gate_slabs(wa, wx, d):
    nb, blk, _ = wa.shape
    per = MXU_DIM // blk
    nslab = d // MXU_DIM

    def slab_diag(w):
        w = w.reshape(nslab, per, blk, blk)
        eye = jnp.eye(per, dtype=w.dtype)
        return jnp.einsum('spij,pq->spiqj', w, eye).reshape(nslab, MXU_DIM, MXU_DIM)

    return jnp.concatenate([slab_diag(wa), slab_diag(wx)], axis=2).astype(BF16)


def _rnn_fwd(xr, conv_w, conv_b, wa, ba, wx, bx, lam):
    b, s, d = xr.shape
    tm = min(RNN_TM, s)
    assert s % tm == 0 and tm % min(RNN_ROW_BLOCK, tm) == 0
    nt = s // tm
    hb = tm // SUBLANES
    wcat = _gate_slabs(wa, wx, d)
    main = pl.BlockSpec((1, tm, d), lambda bi, t: (bi, t, 0))
    const = lambda shape: pl.BlockSpec(shape, lambda bi, t: (0,) * len(shape))
    return pl.pallas_call(
        _rnn_fwd_kernel,
        grid=(b, nt),
        in_specs=[
            main,
            pl.BlockSpec((1, SUBLANES, d), lambda bi, t: (bi, jnp.maximum(t * hb - 1, 0), 0)),
            pl.BlockSpec((1, SUBLANES, d), lambda bi, t: (bi, jnp.minimum((t + 1) * hb, s // SUBLANES - 1), 0)),
            const((CONV_W, d)), const((1, d)), const(wcat.shape), const((1, d)), const((1, d)), const((1, d)),
        ],
        out_specs=[main, main],
        out_shape=[jax.ShapeDtypeStruct((b, s, d), F32), jax.ShapeDtypeStruct((b, s, d), F32)],
        scratch_shapes=[
            pltpu.VMEM((tm + 2 * SUBLANES, d), F32),
            pltpu.VMEM((tm, d), F32),
            pltpu.VMEM((tm, d), F32),
            pltpu.VMEM((1, d), F32),
        ],
        compiler_params=_params("parallel", "arbitrary"),
        name="rnn_fwd",
    )(xr, xr, xr, conv_w, conv_b.reshape(1, d), wcat, ba.reshape(1, d), bx.reshape(1, d), lam.reshape(1, d))


def _rnn_bwd(xc, wa, ba, wx, bx, lam, hf, gy):
    b, s, d = xc.shape
    tm = min(RNN_TM, s)
    assert s % tm == 0 and tm % min(RNN_ROW_BLOCK, tm) == 0
    nt = s // tm
    wcat = _gate_slabs(wa, wx, d)
    main = pl.BlockSpec((1, tm, d), lambda bi, t: (bi, nt - 1 - t, 0))
    const = lambda shape: pl.BlockSpec(shape, lambda bi, t: (0,) * len(shape))
    return pl.pallas_call(
        _rnn_bwd_kernel,
        grid=(b, nt),
        in_specs=[main, const(wcat.shape), const((1, d)), const((1, d)), const((1, d)), main, main],
        out_specs=main,
        out_shape=jax.ShapeDtypeStruct((b, s, d), BF16),
        scratch_shapes=[
            pltpu.VMEM((tm, d), F32),
            pltpu.VMEM((tm, d), F32),
            pltpu.VMEM((tm, d), F32),
            pltpu.VMEM((1, d), F32),
        ],
        compiler_params=_params("parallel", "arbitrary"),
        name="rnn_bwd",
    )(xc, wcat, ba.reshape(1, d), bx.reshape(1, d), lam.reshape(1, d), hf, gy)


def _merge_kernel(x_ref, at_ref, rg_ref, gate_ref, wa_ref, wr_ref, wo_ref, o_ref):
    d = x_ref.shape[1]
    n_tiles, n_kv, hd, gtq = at_ref.shape
    tq = gtq // GROUP
    slab = GROUP * hd
    a_tiles = []
    for t in range(n_tiles):
        a = None
        for j in range(n_kv):
            o_t = at_ref[t, j].astype(F32)
            lhs = jnp.concatenate([o_t[:, g * tq:(g + 1) * tq] for g in range(GROUP)], axis=0).T.astype(BF16)
            term = jnp.dot(lhs, wa_ref[j * slab:(j + 1) * slab, :], preferred_element_type=F32)
            a = term if a is None else a + term
        a_tiles.append(a)
    a = jnp.concatenate(a_tiles, axis=0)
    r = jnp.dot(rg_ref[...], wr_ref[...], preferred_element_type=F32)
    merged = gate_ref[:, :d].astype(F32) * a + gate_ref[:, d:].astype(F32) * r
    o_ref[...] = x_ref[...] + jnp.dot(merged.astype(BF16), wo_ref[...], preferred_element_type=F32)


def _merge(x2, attn_t, rg2, gate2, w_attn_o, w_rnn_o, w_out, layer):
    t, d = x2.shape
    tq = attn_t.shape[4] // GROUP
    tm = min(OUT_TM, t)
    assert t % tm == 0 and tm % tq == 0
    n_tiles = tm // tq
    at = attn_t.reshape((-1,) + attn_t.shape[2:])
    rows = lambda width: pl.BlockSpec((tm, width), lambda i: (i, 0))
    const = lambda shape: pl.BlockSpec((None,) + shape[1:], lambda i: (layer, 0, 0))
    return pl.pallas_call(
        _merge_kernel,
        grid=(t // tm,),
        in_specs=[rows(d), pl.BlockSpec((n_tiles,) + at.shape[1:], lambda i: (i, 0, 0, 0)),
                  rows(rg2.shape[1]), rows(2 * d),
                  const(w_attn_o.shape), const(w_rnn_o.shape), const(w_out.shape)],
        out_specs=rows(d),
        out_shape=jax.ShapeDtypeStruct((t, d), F32),
        compiler_params=_params("parallel"),
        name="merge",
    )(x2, at, rg2, gate2, w_attn_o, w_rnn_o, w_out)


def kernel(x, ffn1_norm, ffn1_w1, ffn1_w2, mix_norm, w_in, b_gate, q_norm, k_norm, w_attn_o, conv_w, conv_b,
           lru_wa, lru_ba, lru_wx, lru_bx, lru_lambda, w_rnn_o, w_out, ffn2_norm, ffn2_w1, ffn2_w2):
    b, s, d = x.shape
    depth = w_in.shape[0]
    x2 = x.reshape(b * s, d)
    cast = lambda w: w.astype(BF16)
    f1w1, f1w2, f2w1, f2w2 = cast(ffn1_w1), cast(ffn1_w2), cast(ffn2_w1), cast(ffn2_w2)
    w_in_b, w_ao, w_ro, w_o = cast(w_in), cast(w_attn_o), cast(w_rnn_o), cast(w_out)
    for l in range(depth):
        x2 = _ffn(x2, ffn1_norm[l], f1w1, f1w2, l)
        qt, k, vt, xr, gy, gates = _in_proj(x2.reshape(b, s, d), mix_norm[l], w_in_b, l, b_gate[l],
                                           q_norm[l], k_norm[l])
        attn = _attention(qt, k, vt)
        hf, xc = _rnn_fwd(xr, conv_w[l], conv_b[l], lru_wa[l, 0], lru_ba[l, 0], lru_wx[l, 0], lru_bx[l, 0],
                          lru_lambda[l, 0])
        rg = _rnn_bwd(xc, lru_wa[l, 1], lru_ba[l, 1], lru_wx[l, 1], lru_bx[l, 1], lru_lambda[l, 1], hf, gy)
        x2 = _merge(x2, attn, rg.reshape(b * s, -1), gates.reshape(b * s, -1), w_ao, w_ro, w_o, l)
        x2 = _ffn(x2, ffn2_norm[l], f2w1, f2w2, l)
    return x2.reshape(b, s, d)
```

```python
import functools
import math

import jax
import jax.numpy as jnp
from jax import lax
from jax.experimental import pallas as pl
from jax.experimental.pallas import tpu as pltpu

F32 = jnp.float32
BF16 = jnp.bfloat16

N_HEADS = 16
N_KV_HEADS = 4
HEAD_DIM = 64
GROUP = N_HEADS // N_KV_HEADS
VT_ROWS = HEAD_DIM + 16
SAFE_SHIFT = 60.0
ROPE_THETA = 10000.0
ROPE_QUARTER = HEAD_DIM // 4
GRID_W = 64
CONV_W = 4
CONV_LEFT = 2
LRU_C = 8.0
FFN_RESID = 0.5
EPS = 1e-6
LOG2E = math.log2(math.e)

LANES = 128
SUBLANES = 8
MXU_DIM = 256
VMEM_LIMIT_BYTES = 56 * 1024 * 1024

FFN_TM = 1024
FFN_TF = 256
PROJ_TM = 512
PROJ_ROW_SPLIT = 2
ATT_TK = 1024
ATT_UNROLL = 8
RNN_TM = 512
RNN_ROW_BLOCK = 256
OUT_TM = 1024


def _params(*sem):
    return pltpu.CompilerParams(dimension_semantics=sem, vmem_limit_bytes=VMEM_LIMIT_BYTES)


def _rms_scale(x):
    return lax.rsqrt(jnp.sum(x * x, axis=-1, keepdims=True) * (1.0 / x.shape[-1]) + EPS)


def _sigmoid(x):
    return 1.0 / (1.0 + jnp.exp2(x * -LOG2E))


def _ffn_kernel(x_ref, g_ref, w1_ref, w2_ref, o_ref):
    d_ff = w2_ref.shape[0]
    x = x_ref[...]
    h = (x * _rms_scale(x) * g_ref[...]).astype(BF16)
    acc = None
    for c in range(0, d_ff, FFN_TF):
        g = jnp.dot(h, w1_ref[:, c:c + FFN_TF], preferred_element_type=F32)
        u = jnp.dot(h, w1_ref[:, d_ff + c:d_ff + c + FFN_TF], preferred_element_type=F32)
        act = (g * _sigmoid(g) * u).astype(BF16)
        part = jnp.dot(act, w2_ref[c:c + FFN_TF, :], preferred_element_type=F32)
        acc = part if acc is None else acc + part
    o_ref[...] = x + FFN_RESID * acc


def _ffn(x2, gain, w1, w2, layer):
    t, d = x2.shape
    tm = min(FFN_TM, t)
    assert t % tm == 0 and w2.shape[1] % FFN_TF == 0 and w1.shape[2] == 2 * w2.shape[1]
    resident = lambda shape: pl.BlockSpec((None,) + shape[1:], lambda i: (layer, 0, 0),
                                          pipeline_mode=pl.Buffered(1))
    return pl.pallas_call(
        _ffn_kernel,
        grid=(t // tm,),
        in_specs=[
            pl.BlockSpec((tm, d), lambda i: (i, 0)),
            pl.BlockSpec((1, d), lambda i: (0, 0)),
            resident(w1.shape),
            resident(w2.shape),
        ],
        out_specs=pl.BlockSpec((tm, d), lambda i: (i, 0)),
        out_shape=jax.ShapeDtypeStruct((t, d), F32),
        compiler_params=_params("parallel"),
        name="ffn",
    )(x2, gain.reshape(1, d), w1, w2)


def _rope_tables(seq):
    pos = jnp.arange(seq, dtype=jnp.int32)
    row = (pos // GRID_W).astype(F32)
    col = (pos % GRID_W).astype(F32)
    half = HEAD_DIM // 2
    inv = ROPE_THETA ** (-jnp.arange(0, half, 2, dtype=F32) / half)
    ang_r = row[:, None] * inv[None, :]
    ang_c = col[:, None] * inv[None, :]
    zero = jnp.zeros_like(ang_r)
    cos = jnp.concatenate([jnp.cos(ang_r), jnp.cos(ang_r), jnp.cos(ang_c), jnp.cos(ang_c)], axis=1)
    s_up = jnp.concatenate([-jnp.sin(ang_r), zero, -jnp.sin(ang_c), zero], axis=1)
    s_dn = jnp.concatenate([zero, jnp.sin(ang_r), zero, jnp.sin(ang_c)], axis=1)
    rep = LANES // HEAD_DIM
    return tuple(jnp.tile(a, (1, rep)) for a in (cos, s_up, s_dn))


def _head_norm_rope(z, seg, gain, cos, s_up, s_dn):
    width = z.shape[1]
    sq = (z * z).astype(BF16)
    ms = jnp.concatenate(
        [jnp.dot(sq[:, c:c + MXU_DIM], seg, preferred_element_type=F32) for c in range(0, width, MXU_DIM)],
        axis=1)
    zn = z * lax.rsqrt(ms + EPS) * gain
    rep = width // LANES
    widen = lambda a: jnp.concatenate([a] * rep, axis=1)
    up = pltpu.roll(zn, width - ROPE_QUARTER, axis=1)
    dn = pltpu.roll(zn, ROPE_QUARTER, axis=1)
    return zn * widen(cos) + up * widen(s_up) + dn * widen(s_dn)


def _proj_kernel(x_ref, gn_ref, w_ref, bg_ref, gq_ref, gk_ref, seg_ref, cos_ref, sup_ref, sdn_ref,
                 qt_ref, k_ref, vt_ref, xr_ref, gy_ref, gate_ref):
    tm = x_ref.shape[1]
    rb = tm // PROJ_ROW_SPLIT
    widths = (N_HEADS * HEAD_DIM, k_ref.shape[1] * HEAD_DIM, k_ref.shape[1] * HEAD_DIM,
              xr_ref.shape[2], gy_ref.shape[2], gate_ref.shape[2])
    starts = [sum(widths[:i]) for i in range(len(widths))]
    wq_ref, wk_ref, wv_ref, wx_ref, wy_ref, wg_ref = (w_ref.at[:, o:o + n] for o, n in zip(starts, widths))
    seg = seg_ref[...]
    qscale = HEAD_DIM ** -0.5 * LOG2E
    for blk in range(PROJ_ROW_SPLIT):
        rows = slice(blk * rb, (blk + 1) * rb)
        x = x_ref[0, rows, :]
        h = (x * _rms_scale(x) * gn_ref[...]).astype(BF16)
        cos, s_up, s_dn = cos_ref[rows, :], sup_ref[rows, :], sdn_ref[rows, :]

        q = jnp.dot(h, wq_ref[...], preferred_element_type=F32)
        q_t = (_head_norm_rope(q, seg, gq_ref[...], cos, s_up, s_dn) * qscale).T.astype(BF16)
        for j in range(N_KV_HEADS):
            for g in range(GROUP):
                r0 = (j * GROUP + g) * HEAD_DIM
                qt_ref[0, 0, j, :, g * tm + blk * rb:g * tm + (blk + 1) * rb] = q_t[r0:r0 + HEAD_DIM, :]

        k = jnp.dot(h, wk_ref[...], preferred_element_type=F32)
        k = _head_norm_rope(k, seg, gk_ref[...], cos, s_up, s_dn).astype(BF16)
        vt = jnp.dot(h, wv_ref[...], preferred_element_type=F32).T.astype(BF16)
        for j in range(N_KV_HEADS):
            k_ref[0, j, rows, :] = k[:, j * HEAD_DIM:(j + 1) * HEAD_DIM]
            vt_ref[0, j, 0, 0:HEAD_DIM, rows] = vt[j * HEAD_DIM:(j + 1) * HEAD_DIM, :]
            vt_ref[0, j, 0, HEAD_DIM:VT_ROWS, rows] = jnp.ones((VT_ROWS - HEAD_DIM, rb), BF16)

        xr_ref[0, rows, :] = jnp.dot(h, wx_ref[...], preferred_element_type=F32)

        y = jnp.dot(h, wy_ref[...], preferred_element_type=F32)
        gelu = 0.5 * y * (1.0 + jnp.tanh(math.sqrt(2.0 / math.pi) * (y + 0.044715 * (y * y * y))))
        gy_ref[0, rows, :] = gelu.astype(BF16)

        gl = jnp.dot(h, wg_ref[...], preferred_element_type=F32) + bg_ref[...]
        gate_ref[0, rows, :] = _sigmoid(gl).astype(BF16)


def _in_proj(x, gain, w_in, layer, b_gate, q_gain, k_gain):
    b, s, d = x.shape
    q_cols = N_HEADS * HEAD_DIM
    kv_cols = N_KV_HEADS * HEAD_DIM
    d_rnn = (w_in.shape[2] - q_cols - 2 * kv_cols - 2 * d) // 2
    tm = min(PROJ_TM, s)
    nt = s // tm
    assert s % tm == 0 and tm % (PROJ_ROW_SPLIT * LANES) == 0 and s % GRID_W == 0
    cos, s_up, s_dn = _rope_tables(s)
    lane = jnp.arange(MXU_DIM)
    seg = jnp.where(lane[:, None] // HEAD_DIM == lane[None, :] // HEAD_DIM, 1.0 / HEAD_DIM, 0.0).astype(BF16)
    const = lambda shape: pl.BlockSpec(shape, lambda bi, ti: (0,) * len(shape))
    tab = pl.BlockSpec((tm, LANES), lambda bi, ti: (ti, 0))
    row = lambda width: pl.BlockSpec((1, tm, width), lambda bi, ti: (bi, ti, 0))
    head = pl.BlockSpec((1, N_KV_HEADS, tm, HEAD_DIM), lambda bi, ti: (bi, 0, ti, 0))
    head_t = pl.BlockSpec((1, N_KV_HEADS, 1, VT_ROWS, tm), lambda bi, ti: (bi, 0, ti, 0, 0))
    return pl.pallas_call(
        _proj_kernel,
        grid=(b, nt),
        in_specs=[row(d), const((1, d)),
                  pl.BlockSpec((None,) + w_in.shape[1:], lambda bi, ti: (layer, 0, 0), pipeline_mode=pl.Buffered(1)),
                  const((1, 2 * d)), const((1, q_cols)), const((1, kv_cols)), const((MXU_DIM, MXU_DIM)), tab, tab, tab],
        out_specs=[pl.BlockSpec((1, 1, N_KV_HEADS, HEAD_DIM, GROUP * tm), lambda bi, ti: (bi, ti, 0, 0, 0)),
                   head, head_t, row(d_rnn), row(d_rnn), row(2 * d)],
        out_shape=[
            jax.ShapeDtypeStruct((b, nt, N_KV_HEADS, HEAD_DIM, GROUP * tm), BF16),
            jax.ShapeDtypeStruct((b, N_KV_HEADS, s, HEAD_DIM), BF16),
            jax.ShapeDtypeStruct((b, N_KV_HEADS, nt, VT_ROWS, tm), BF16),
            jax.ShapeDtypeStruct((b, s, d_rnn), F32),
            jax.ShapeDtypeStruct((b, s, d_rnn), BF16),
            jax.ShapeDtypeStruct((b, s, 2 * d), BF16),
        ],
        compiler_params=_params("parallel", "parallel"),
        name="in_proj",
    )(x, gain.reshape(1, d), w_in, b_gate.reshape(1, 2 * d),
      jnp.tile(q_gain, N_HEADS).reshape(1, q_cols), jnp.tile(k_gain, N_KV_HEADS).reshape(1, kv_cols),
      seg, cos, s_up, s_dn)


def _attn_kernel(q_ref, k_ref, vt_ref, o_ref, shift_ref, m_ref, acc_ref, kmax_ref, *, tk):
    qt = q_ref.at[0, 0, 0]
    seq = k_ref.shape[2]
    kc = vt_ref.shape[4]
    nsub = tk // kc
    nchunk = seq // tk

    @pl.when(pl.program_id(2) == 0)
    def _():
        def norm_chunk(c, best):
            kk = k_ref[0, 0, pl.ds(pl.multiple_of(c * tk, tk), tk), :].astype(F32)
            return jnp.maximum(best, jnp.max(jnp.sum(kk * kk, axis=1, keepdims=True), axis=0, keepdims=True))
        kmax_ref[...] = jnp.sqrt(lax.fori_loop(0, nchunk, norm_chunk, jnp.zeros((1, 1), F32)))

    qf = qt[...].astype(F32)
    shift_ref[...] = jnp.sqrt(jnp.sum(qf * qf, axis=0, keepdims=True)) * kmax_ref[...]
    acc_ref[...] = jnp.zeros_like(acc_ref)
    safe = jnp.max(shift_ref[...]) <= SAFE_SHIFT

    def pv(c, p_t):
        vt = jnp.concatenate([vt_ref[0, 0, c * nsub + sub] for sub in range(nsub)], axis=1)
        return jnp.dot(vt, p_t, preferred_element_type=F32)

    def scores(c):
        k = k_ref[0, 0, pl.ds(pl.multiple_of(c * tk, tk), tk), :]
        return jnp.dot(k, qt[...], preferred_element_type=F32)

    @pl.when(safe)
    def _():
        def body(c, carry):
            p_t = jnp.exp2(scores(c) - shift_ref[...]).astype(BF16)
            acc_ref[...] += pv(c, p_t)
            return carry
        lax.fori_loop(0, nchunk, body, 0, unroll=min(ATT_UNROLL, nchunk))

    @pl.when(jnp.logical_not(safe))
    def _():
        m_ref[...] = jnp.full_like(m_ref, -jnp.inf)

        def body(c, carry):
            s_t = scores(c)
            m_prev = m_ref[...]
            m_new = jnp.maximum(m_prev, jnp.max(s_t, axis=0, keepdims=True))
            p_t = jnp.exp2(s_t - m_new).astype(BF16)
            acc_ref[...] = jnp.exp2(m_prev - m_new) * acc_ref[...] + pv(c, p_t)
            m_ref[...] = m_new
            return carry
        lax.fori_loop(0, nchunk, body, 0)

    o_ref[0, 0, 0] = (acc_ref[0:HEAD_DIM, :] / acc_ref[HEAD_DIM:HEAD_DIM + 1, :]).astype(o_ref.dtype)


def _attention(qt, k, vt):
    b, nq, _, _, gtq = qt.shape
    s = k.shape[2]
    tk = min(ATT_TK, s)
    nchunk, vt_rows, kc = vt.shape[2:]
    assert s % tk == 0 and tk % kc == 0 and (s // tk) % min(ATT_UNROLL, s // tk) == 0
    tile = pl.BlockSpec((1, 1, 1, HEAD_DIM, gtq), lambda bi, j, i: (bi, i, j, 0, 0))
    return pl.pallas_call(
        functools.partial(_attn_kernel, tk=tk),
        grid=(b, N_KV_HEADS, nq),
        in_specs=[
            tile,
            pl.BlockSpec((1, 1, s, HEAD_DIM), lambda bi, j, i: (bi, j, 0, 0)),
            pl.BlockSpec((1, 1, nchunk, vt_rows, kc), lambda bi, j, i: (bi, j, 0, 0, 0)),
        ],
        out_specs=tile,
        out_shape=jax.ShapeDtypeStruct(qt.shape, BF16),
        scratch_shapes=[
            pltpu.VMEM((1, gtq), F32),
            pltpu.VMEM((1, gtq), F32),
            pltpu.VMEM((vt_rows, gtq), F32),
            pltpu.VMEM((1, 1), F32),
        ],
        compiler_params=_params("parallel", "arbitrary", "arbitrary"),
        name="attention",
    )(qt, k, vt)


def _lru_gates_and_scan(xc, wcat_ref, ba_ref, bx_ref, lam_ref, a_ref, u_ref, hs_ref, hc_ref, reverse):
    tm, d = xc.shape
    rb = min(RNN_ROW_BLOCK, tm)
    lam = lam_ref[...]
    softplus_neg_lam = jnp.maximum(-lam, 0.0) + jnp.log1p(jnp.exp(-jnp.abs(lam)))
    decay_rate = (-LRU_C * LOG2E) * softplus_neg_lam
    h = hc_ref[...]
    order = range(tm // rb - 1, -1, -1) if reverse else range(tm // rb)
    for blk in order:
        rows = slice(blk * rb, (blk + 1) * rb)
        xb = xc[rows, :]
        xbb = xb.astype(BF16)
        for c in range(d // MXU_DIM):
            sl = slice(c * MXU_DIM, (c + 1) * MXU_DIM)
            g = jnp.dot(xbb[:, sl], wcat_ref[c], preferred_element_type=F32)
            r = _sigmoid(g[:, :MXU_DIM] + ba_ref[:, sl])
            i = _sigmoid(g[:, MXU_DIM:] + bx_ref[:, sl])
            a = jnp.exp2(r * decay_rate[:, sl])
            a_ref[rows, sl] = a
            y = 1.0 - a * a
            u_ref[rows, sl] = y * lax.rsqrt(jnp.maximum(y, 1e-30)) * (i * xb[:, sl])
        for row in (range((blk + 1) * rb - 1, blk * rb - 1, -1) if reverse else range(blk * rb, (blk + 1) * rb)):
            h = a_ref[row:row + 1, :] * h + u_ref[row:row + 1, :]
            hs_ref[row:row + 1, :] = h
    hc_ref[...] = h


def _rnn_fwd_kernel(xr_ref, prev_ref, next_ref, cw_ref, cb_ref, wcat_ref, ba_ref, bx_ref, lam_ref,
                    hf_ref, xc_ref, xpad_ref, a_ref, u_ref, hc_ref):
    t = pl.program_id(1)
    nt = pl.num_programs(1)
    tm = xr_ref.shape[1]
    halo = SUBLANES

    @pl.when(t == 0)
    def _():
        hc_ref[...] = jnp.zeros_like(hc_ref)

    xpad_ref[halo:halo + tm, :] = xr_ref[0]
    xpad_ref[0:halo, :] = jnp.where(t > 0, prev_ref[0], 0.0)
    xpad_ref[halo + tm:halo + tm + halo, :] = jnp.where(t < nt - 1, next_ref[0], 0.0)
    xc = cb_ref[...] + sum(
        cw_ref[kk:kk + 1, :] * xpad_ref[halo - CONV_LEFT + kk:halo - CONV_LEFT + kk + tm, :]
        for kk in range(CONV_W))
    xc_ref[0] = xc
    _lru_gates_and_scan(xc, wcat_ref, ba_ref, bx_ref, lam_ref, a_ref, u_ref, hf_ref.at[0], hc_ref, reverse=False)


def _rnn_bwd_kernel(xc_ref, wcat_ref, ba_ref, bx_ref, lam_ref, hf_ref, gy_ref, o_ref, a_ref, u_ref, hs_ref, hc_ref):
    @pl.when(pl.program_id(1) == 0)
    def _():
        hc_ref[...] = jnp.zeros_like(hc_ref)

    _lru_gates_and_scan(xc_ref[0], wcat_ref, ba_ref, bx_ref, lam_ref, a_ref, u_ref, hs_ref, hc_ref, reverse=True)
    o_ref[0] = ((hf_ref[0] + hs_ref[...]) * gy_ref[0].astype(F32)).astype(o_ref.dtype)


def _gate_slabs(wa, wx, d):
    nb, blk, _ = wa.shape
    per = MXU_DIM // blk
    nslab = d // MXU_DIM

    def slab_diag(w):
        w = w.reshape(nslab, per, blk, blk)
        eye = jnp.eye(per, dtype=w.dtype)
        return jnp.einsum('spij,pq->spiqj', w, eye).reshape(nslab, MXU_DIM, MXU_DIM)

    return jnp.concatenate([slab_diag(wa), slab_diag(wx)], axis=2).astype(BF16)


def _rnn_fwd(xr, conv_w, conv_b, wa, ba, wx, bx, lam):
    b, s, d = xr.shape
    tm = min(RNN_TM, s)
    assert s % tm == 0 and tm % min(RNN_ROW_BLOCK, tm) == 0
    nt = s // tm
    hb = tm // SUBLANES
    wcat = _gate_slabs(wa, wx, d)
    main = pl.BlockSpec((1, tm, d), lambda bi, t: (bi, t, 0))
    const = lambda shape: pl.BlockSpec(shape, lambda bi, t: (0,) * len(shape))
    return pl.pallas_call(
        _rnn_fwd_kernel,
        grid=(b, nt),
        in_specs=[
            main,
            pl.BlockSpec((1, SUBLANES, d), lambda bi, t: (bi, jnp.maximum(t * hb - 1, 0), 0)),
            pl.BlockSpec((1, SUBLANES, d), lambda bi, t: (bi, jnp.minimum((t + 1) * hb, s // SUBLANES - 1), 0)),
            const((CONV_W, d)), const((1, d)), const(wcat.shape), const((1, d)), const((1, d)), const((1, d)),
        ],
        out_specs=[main, main],
        out_shape=[jax.ShapeDtypeStruct((b, s, d), F32), jax.ShapeDtypeStruct((b, s, d), F32)],
        scratch_shapes=[
            pltpu.VMEM((tm + 2 * SUBLANES, d), F32),
            pltpu.VMEM((tm, d), F32),
            pltpu.VMEM((tm, d), F32),
            pltpu.VMEM((1, d), F32),
        ],
        compiler_params=_params("parallel", "arbitrary"),
        name="rnn_fwd",
    )(xr, xr, xr, conv_w, conv_b.reshape(1, d), wcat, ba.reshape(1, d), bx.reshape(1, d), lam.reshape(1, d))


def _rnn_bwd(xc, wa, ba, wx, bx, lam, hf, gy):
    b, s, d = xc.shape
    tm = min(RNN_TM, s)
    assert s % tm == 0 and tm % min(RNN_ROW_BLOCK, tm) == 0
    nt = s // tm
    wcat = _gate_slabs(wa, wx, d)
    main = pl.BlockSpec((1, tm, d), lambda bi, t: (bi, nt - 1 - t, 0))
    const = lambda shape: pl.BlockSpec(shape, lambda bi, t: (0,) * len(shape))
    return pl.pallas_call(
        _rnn_bwd_kernel,
        grid=(b, nt),
        in_specs=[main, const(wcat.shape), const((1, d)), const((1, d)), const((1, d)), main, main],
        out_specs=main,
        out_shape=jax.ShapeDtypeStruct((b, s, d), BF16),
        scratch_shapes=[
            pltpu.VMEM((tm, d), F32),
            pltpu.VMEM((tm, d), F32),
            pltpu.VMEM((tm, d), F32),
            pltpu.VMEM((1, d), F32),
        ],
        compiler_params=_params("parallel", "arbitrary"),
        name="rnn_bwd",
    )(xc, wcat, ba.reshape(1, d), bx.reshape(1, d), lam.reshape(1, d), hf, gy)


def _merge_kernel(x_ref, at_ref, rg_ref, gate_ref, wa_ref, wr_ref, wo_ref, o_ref):
    d = x_ref.shape[1]
    n_tiles, n_kv, hd, gtq = at_ref.shape
    tq = gtq // GROUP
    slab = GROUP * hd
    a_tiles = []
    for t in range(n_tiles):
        a = None
        for j in range(n_kv):
            o_t = at_ref[t, j].astype(F32)
            lhs = jnp.concatenate([o_t[:, g * tq:(g + 1) * tq] for g in range(GROUP)], axis=0).T.astype(BF16)
            term = jnp.dot(lhs, wa_ref[j * slab:(j + 1) * slab, :], preferred_element_type=F32)
            a = term if a is None else a + term
        a_tiles.append(a)
    a = jnp.concatenate(a_tiles, axis=0)
    r = jnp.dot(rg_ref[...], wr_ref[...], preferred_element_type=F32)
    merged = gate_ref[:, :d].astype(F32) * a + gate_ref[:, d:].astype(F32) * r
    o_ref[...] = x_ref[...] + jnp.dot(merged.astype(BF16), wo_ref[...], preferred_element_type=F32)


def _merge(x2, attn_t, rg2, gate2, w_attn_o, w_rnn_o, w_out, layer):
    t, d = x2.shape
    tq = attn_t.shape[4] // GROUP
    tm = min(OUT_TM, t)
    assert t % tm == 0 and tm % tq == 0
    n_tiles = tm // tq
    at = attn_t.reshape((-1,) + attn_t.shape[2:])
    rows = lambda width: pl.BlockSpec((tm, width), lambda i: (i, 0))
    const = lambda shape: pl.BlockSpec((None,) + shape[1:], lambda i: (layer, 0, 0))
    return pl.pallas_call(
        _merge_kernel,
        grid=(t // tm,),
        in_specs=[rows(d), pl.BlockSpec((n_tiles,) + at.shape[1:], lambda i: (i, 0, 0, 0)),
                  rows(rg2.shape[1]), rows(2 * d),
                  const(w_attn_o.shape), const(w_rnn_o.shape), const(w_out.shape)],
        out_specs=rows(d),
        out_shape=jax.ShapeDtypeStruct((t, d), F32),
        compiler_params=_params("parallel"),
        name="merge",
    )(x2, at, rg2, gate2, w_attn_o, w_rnn_o, w_out)


def kernel(x, ffn1_norm, ffn1_w1, ffn1_w2, mix_norm, w_in, b_gate, q_norm, k_norm, w_attn_o, conv_w, conv_b,
           lru_wa, lru_ba, lru_wx, lru_bx, lru_lambda, w_rnn_o, w_out, ffn2_norm, ffn2_w1, ffn2_w2):
    b, s, d = x.shape
    depth = w_in.shape[0]
    x2 = x.reshape(b * s, d)
    cast = lambda w: w.astype(BF16)
    f1w1, f1w2, f2w1, f2w2 = cast(ffn1_w1), cast(ffn1_w2), cast(ffn2_w1), cast(ffn2_w2)
    w_in_b, w_ao, w_ro, w_o = cast(w_in), cast(w_attn_o), cast(w_rnn_o), cast(w_out)
    for l in range(depth):
        x2 = _ffn(x2, ffn1_norm[l], f1w1, f1w2, l)
        qt, k, vt, xr, gy, gates = _in_proj(x2.reshape(b, s, d), mix_norm[l], w_in_b, l, b_gate[l],
                                           q_norm[l], k_norm[l])
        attn = _attention(qt, k, vt)
        hf, xc = _rnn_fwd(xr, conv_w[l], conv_b[l], lru_wa[l, 0], lru_ba[l, 0], lru_wx[l, 0], lru_bx[l, 0],
                          lru_lambda[l, 0])
        rg = _rnn_bwd(xc, lru_wa[l, 1], lru_ba[l, 1], lru_wx[l, 1], lru_bx[l, 1], lru_lambda[l, 1], hf, gy)
        x2 = _merge(x2, attn, rg.reshape(b * s, -1), gates.reshape(b * s, -1), w_ao, w_ro, w_o, l)
        x2 = _ffn(x2, ffn2_norm[l], f2w1, f2w2, l)
    return x2.reshape(b, s, d)
```
